```python
import jax, jax.numpy as jnp
from jax import lax
import numpy as np

D_MODEL = 2048
BATCH = 4
SEQ = 4096
DEPTH = 4
DEC_BATCH = 32
DEC_SEQ = 32
PAST_LEN = 2048

CHUNK = 64
HEAD_DIM = 128
MEM_HEADS = 4
MEM_DIM = MEM_HEADS * HEAD_DIM
TOK_DIM = D_MODEL - MEM_DIM
TOK_HEADS = TOK_DIM // HEAD_DIM
N_MEM = 256
D_FF = 4 * D_MODEL
CONV_W = 4
N_A = DEPTH // 2
N_B = DEPTH - N_A
Q_BLOCK = 128
EPS = 1e-6
QKV_DIM = 3 * TOK_DIM
IN_A = QKV_DIM + TOK_DIM + 2 * TOK_HEADS + MEM_DIM
IN_B = 2 * TOK_DIM + MEM_DIM
KVF_DIM = 2 * TOK_DIM + TOK_HEADS
SCALE = HEAD_DIM ** -0.5
F32 = jnp.float32

kernel_name = "yoco_gdn_fox_stream_step"


def rmsnorm(x, g):
    xf = x.astype(F32)
    y = xf * lax.rsqrt(jnp.mean(xf * xf, axis=-1, keepdims=True) + EPS)
    return (y * g.astype(F32)).astype(x.dtype)


def l2norm(x):
    xf = x.astype(F32)
    return xf * lax.rsqrt(jnp.sum(xf * xf, axis=-1, keepdims=True) + EPS)


def causal_conv(x, hist, w):
    T = x.shape[1]
    xp = jnp.concatenate([hist.astype(x.dtype), x], axis=1)
    y = xp[:, 0:T] * w[0]
    for j in range(1, CONV_W):
        y = y + xp[:, j:j + T] * w[j]
    return y, xp[:, T:]


def gdn_chunk(S, inp):
    q, k, v, g, beta = inp
    L = q.shape[2]
    dv = v.shape[-1]
    idx = jnp.arange(L)
    incl = idx[:, None] >= idx[None, :]
    strict = idx[:, None] > idx[None, :]
    G = jnp.cumsum(g, axis=-1)
    decay = jnp.exp(jnp.where(incl, G[..., :, None] - G[..., None, :], -jnp.inf))
    A = jnp.where(strict, beta[..., :, None] * jnp.einsum('bhid,bhjd->bhij', k, k) * decay, 0.0)
    eye = jnp.eye(L, dtype=F32)
    rhs = jnp.concatenate([beta[..., None] * v, (beta * jnp.exp(G))[..., None] * k], axis=-1)
    sol = lax.linalg.triangular_solve(eye + A, rhs, left_side=True, lower=True, unit_diagonal=True)
    u, w = sol[..., :dv], sol[..., dv:]
    v_new = u - jnp.einsum('bhik,bhkv->bhiv', w, S)
    attn = jnp.einsum('bhid,bhjd->bhij', q, k) * decay
    out = (jnp.einsum('bhik,bhkv->bhiv', q * jnp.exp(G)[..., None], S)
           + jnp.einsum('bhij,bhjv->bhiv', attn, v_new))
    GL = G[..., -1:]
    S_new = (jnp.exp(GL)[..., None] * S
             + jnp.einsum('bhik,bhiv->bhkv', k * jnp.exp(GL - G)[..., None], v_new))
    return S_new, out


def gated_delta(q, k, v, g, beta, S0):
    B, T, H, D = q.shape
    L = min(T, CHUNK)
    N = T // L

    def blocks(a):
        a = a.reshape((B, N, L) + a.shape[2:])
        return a.transpose((1, 0, 3, 2) + tuple(range(4, a.ndim)))

    S_fin, outs = lax.scan(gdn_chunk, S0, (blocks(q), blocks(k), blocks(v), blocks(g), blocks(beta)))
    o = outs.transpose(1, 0, 3, 2, 4).reshape(B, T, H, D)
    return o, S_fin


def gdn_mixer(h, conv_hist, s0, w_in, conv_w, a_log, dt_bias, gdn_norm):
    B, T, _ = h.shape
    proj = h @ w_in
    o1 = QKV_DIM + TOK_DIM
    qkv = proj[..., :QKV_DIM]
    z = proj[..., QKV_DIM:o1]
    a = proj[..., o1:o1 + TOK_HEADS]
    b = proj[..., o1 + TOK_HEADS:o1 + 2 * TOK_HEADS]
    mq = proj[..., o1 + 2 * TOK_HEADS:]
    qkv, conv_state = causal_conv(qkv, conv_hist, conv_w)
    qkv = jax.nn.silu(qkv).reshape(B, T, 3, TOK_HEADS, HEAD_DIM)
    q = l2norm(qkv[:, :, 0]) * SCALE
    k = l2norm(qkv[:, :, 1])
    v = qkv[:, :, 2].astype(F32)
    g = -jnp.exp(a_log.astype(F32)) * jax.nn.softplus(a.astype(F32) + dt_bias.astype(F32))
    beta = jax.nn.sigmoid(b.astype(F32))
    o, s_new = gated_delta(q, k, v, g, beta, s0.astype(F32))
    o = rmsnorm(o, gdn_norm) * jax.nn.silu(z.reshape(B, T, TOK_HEADS, HEAD_DIM).astype(F32))
    return o.reshape(B, T, TOK_DIM).astype(h.dtype), conv_state, s_new, mq


def fox_block(q, cq, qpos, k, v, ckT, kpos):
    s = jnp.einsum('bqhd,bkhd->bhqk', q, k).astype(F32) * SCALE
    s = s + jnp.swapaxes(cq, 1, 2)[..., :, None] - ckT[..., None, :]
    s = jnp.where(kpos[None, :] <= qpos[:, None], s, -jnp.inf)
    p = jax.nn.softmax(s, axis=-1).astype(v.dtype)
    return jnp.einsum('bhqk,bkhd->bqhd', p, v)


def fox_attend(q, cq, qpos, k, v, ck, kpos):
    B, Tq, H, D = q.shape
    ckT = jnp.swapaxes(ck, 1, 2)
    if Tq <= Q_BLOCK:
        return fox_block(q, cq, qpos, k, v, ckT, kpos)
    nb = Tq // Q_BLOCK
    qb = jnp.swapaxes(q.reshape(B, nb, Q_BLOCK, H, D), 0, 1)
    cqb = jnp.swapaxes(cq.reshape(B, nb, Q_BLOCK, H), 0, 1)
    pb = qpos.reshape(nb, Q_BLOCK)
    out = lax.map(lambda a: fox_block(a[0], a[1], a[2], k, v, ckT, kpos), (qb, cqb, pb))
    return jnp.swapaxes(out, 0, 1).reshape(B, Tq, H, D)


def fox_mixer(h, w_in, k_all, v_all, c_all, qpos, kpos, P):
    B, T, _ = h.shape
    proj = h @ w_in
    q = proj[..., :TOK_DIM].reshape(B, T, TOK_HEADS, HEAD_DIM)
    gate = proj[..., TOK_DIM:2 * TOK_DIM]
    mq = proj[..., 2 * TOK_DIM:]
    o = fox_attend(q, c_all[:, P:], qpos, k_all, v_all, c_all, kpos)
    return o.reshape(B, T, TOK_DIM) * jax.nn.sigmoid(gate), mq


def shared_kv(x, norm_kv, w_kvf, b_f):
    B, T, _ = x.shape
    kvf = rmsnorm(x, norm_kv) @ w_kvf
    k = kvf[..., :TOK_DIM].reshape(B, T, TOK_HEADS, HEAD_DIM)
    v = kvf[..., TOK_DIM:2 * TOK_DIM].reshape(B, T, TOK_HEADS, HEAD_DIM)
    logf = jax.nn.log_sigmoid(kvf[..., 2 * TOK_DIM:].astype(F32) + b_f.astype(F32))
    return k, v, logf


def memory_kv(mem, norm_mem, w_mem_kv):
    hm = rmsnorm(mem[None], norm_mem[:, None, None, :])
    kv = jnp.einsum('lbnd,lde->lbne', hm, w_mem_kv)
    B, N = mem.shape[0], mem.shape[1]
    mk = kv[..., :MEM_DIM].reshape(DEPTH, B, N, MEM_HEADS, HEAD_DIM)
    mv = kv[..., MEM_DIM:].reshape(DEPTH, B, N, MEM_HEADS, HEAD_DIM)
    return mk, mv


def mem_attend(mq, mk, mv):
    B, T, _ = mq.shape
    q = mq.reshape(B, T, MEM_HEADS, HEAD_DIM)
    s = jnp.einsum('bthd,bnhd->bhtn', q, mk).astype(F32) * SCALE
    p = jax.nn.softmax(s, axis=-1).astype(mv.dtype)
    return jnp.einsum('bhtn,bnhd->bthd', p, mv).reshape(B, T, MEM_DIM)


def trunk(x, conv_hist, gdn_s0, past_k, past_v, past_logf, mem_k, mem_v, p):
    B, T, _ = x.shape
    P = past_k.shape[1]
    conv_out, gdn_out = [], []
    k_new = v_new = logf_new = None
    for l in range(DEPTH):
        if l == N_A:
            k_new, v_new, logf_new = shared_kv(x, p['norm_kv'], p['w_kvf'], p['b_f'])
            k_all = jnp.concatenate([past_k.astype(k_new.dtype), k_new], axis=1)
            v_all = jnp.concatenate([past_v.astype(v_new.dtype), v_new], axis=1)
            c_all = jnp.cumsum(jnp.concatenate([past_logf.astype(F32), logf_new], axis=1), axis=1)
            kpos = jnp.arange(P + T)
            qpos = P + jnp.arange(T)
        h = rmsnorm(x, p['norm_mix_pre'][l])
        if l < N_A:
            tok, cs, gs, mq = gdn_mixer(h, conv_hist[l], gdn_s0[l], p['w_in_a'][l], p['conv_w_a'][l],
                                        p['a_log'][l], p['dt_bias'][l], p['gdn_norm'][l])
            conv_out.append(cs)
            gdn_out.append(gs)
        else:
            tok, mq = fox_mixer(h, p['w_in_b'][l - N_A], k_all, v_all, c_all, qpos, kpos, P)
        mem_o = mem_attend(mq, mem_k[l], mem_v[l])
        mix = jnp.concatenate([tok, mem_o], axis=-1) @ p['w_o'][l]
        x = x + rmsnorm(mix, p['norm_mix_post'][l])
        hf = rmsnorm(x, p['norm_mlp_pre'][l])
        f = jnp.square(jax.nn.relu(hf @ p['w_up'][l])) @ p['w_down'][l]
        x = x + rmsnorm(f, p['norm_mlp_post'][l])
    return x, jnp.stack(conv_out), jnp.stack(gdn_out), k_new, v_new, logf_new


def setup_inputs(seed: int = 0) -> dict:
    key = jax.random.key(seed)
    ks = jax.random.split(key, 32)

    def nrm(k, shape, scale):
        return jax.random.normal(k, shape, F32) * scale

    def gain(k, shape):
        return 1.0 + 0.02 * jax.random.normal(k, shape, F32)

    dt = jnp.exp(jax.random.uniform(ks[7], (N_A, TOK_HEADS), F32, float(np.log(1e-3)), float(np.log(1e-1))))
    w_kvf = nrm(ks[12], (D_MODEL, KVF_DIM), D_MODEL ** -0.5)
    w_kvf = w_kvf.at[:, 2 * TOK_DIM:].multiply(0.3)
    logf_bias = jax.random.uniform(ks[20], (1, 1, TOK_HEADS), F32, 2.0, 6.0)
    return {
        "x_prompt": nrm(ks[0], (BATCH, SEQ, D_MODEL), 1.0),
        "x_sample": nrm(ks[1], (DEC_BATCH, DEC_SEQ, D_MODEL), 1.0),
        "state_gdn": nrm(ks[17], (N_A, DEC_BATCH, TOK_HEADS, HEAD_DIM, HEAD_DIM), 0.1),
        "state_conv": nrm(ks[18], (N_A, DEC_BATCH, CONV_W - 1, QKV_DIM), 1.0),
        "cache_k": nrm(ks[19], (DEC_BATCH, PAST_LEN, TOK_HEADS, HEAD_DIM), 1.0),
        "cache_v": nrm(ks[21], (DEC_BATCH, PAST_LEN, TOK_HEADS, HEAD_DIM), 1.0),
        "cache_logf": jax.nn.log_sigmoid(logf_bias + 0.3 * jax.random.normal(ks[22], (DEC_BATCH, PAST_LEN, TOK_HEADS), F32)),
        "cache_mem_k": nrm(ks[23], (DEPTH, DEC_BATCH, N_MEM, MEM_HEADS, HEAD_DIM), 1.0),
        "cache_mem_v": nrm(ks[24], (DEPTH, DEC_BATCH, N_MEM, MEM_HEADS, HEAD_DIM), 1.0),
        "mem_prompt": nrm(ks[2], (BATCH, N_MEM, D_MODEL), 1.0),
        "norm_mix_pre": gain(ks[3], (DEPTH, D_MODEL)),
        "norm_mix_post": gain(ks[4], (DEPTH, D_MODEL)),
        "norm_mlp_pre": gain(ks[5], (DEPTH, D_MODEL)),
        "norm_mlp_post": gain(ks[6], (DEPTH, D_MODEL)),
        "w_in_a": nrm(ks[8], (N_A, D_MODEL, IN_A), D_MODEL ** -0.5),
        "conv_w_a": nrm(ks[9], (N_A, CONV_W, QKV_DIM), CONV_W ** -0.5),
        "a_log": jnp.log(jax.random.uniform(ks[10], (N_A, TOK_HEADS), F32, 1.0, 16.0)),
        "dt_bias": dt + jnp.log(-jnp.expm1(-dt)),
        "gdn_norm": gain(ks[11], (N_A, HEAD_DIM)),
        "w_in_b": nrm(ks[13], (N_B, D_MODEL, IN_B), D_MODEL ** -0.5),
        "norm_kv": gain(ks[14], (D_MODEL,)),
        "w_kvf": w_kvf,
        "b_f": jax.random.uniform(ks[15], (TOK_HEADS,), F32, 2.0, 6.0),
        "norm_mem": gain(ks[16], (DEPTH, D_MODEL)),
        "w_mem_kv": nrm(ks[25], (DEPTH, D_MODEL, 2 * MEM_DIM), D_MODEL ** -0.5),
        "w_o": nrm(ks[26], (DEPTH, D_MODEL, D_MODEL), D_MODEL ** -0.5),
        "w_up": nrm(ks[27], (DEPTH, D_MODEL, D_FF), D_MODEL ** -0.5),
        "w_down": nrm(ks[28], (DEPTH, D_FF, D_MODEL), D_FF ** -0.5),
    }


def reference(x_prompt, x_sample, state_gdn, state_conv, cache_k, cache_v, cache_logf, cache_mem_k, cache_mem_v,
              mem_prompt, norm_mix_pre, norm_mix_post, norm_mlp_pre, norm_mlp_post, w_in_a, conv_w_a, a_log,
              dt_bias, gdn_norm, w_in_b, norm_kv, w_kvf, b_f, norm_mem, w_mem_kv, w_o, w_up, w_down):
    params = dict(norm_mix_pre=norm_mix_pre, norm_mix_post=norm_mix_post, norm_mlp_pre=norm_mlp_pre,
                  norm_mlp_post=norm_mlp_post, w_in_a=w_in_a, conv_w_a=conv_w_a, a_log=a_log, dt_bias=dt_bias,
                  gdn_norm=gdn_norm, w_in_b=w_in_b, norm_kv=norm_kv, w_kvf=w_kvf, b_f=b_f, w_o=w_o,
                  w_up=w_up, w_down=w_down)
    Bp = x_prompt.shape[0]
    dtp = x_prompt.dtype
    p_mem_k, p_mem_v = memory_kv(mem_prompt, norm_mem, w_mem_kv)
    zero_conv = jnp.zeros((N_A, Bp, CONV_W - 1, QKV_DIM), dtp)
    zero_gdn = jnp.zeros((N_A, Bp, TOK_HEADS, HEAD_DIM, HEAD_DIM), F32)
    empty_kv = jnp.zeros((Bp, 0, TOK_HEADS, HEAD_DIM), dtp)
    empty_logf = jnp.zeros((Bp, 0, TOK_HEADS), F32)
    y_prompt, p_conv, p_gdn, p_k, p_v, p_logf = trunk(
        x_prompt, zero_conv, zero_gdn, empty_kv, empty_kv, empty_logf, p_mem_k, p_mem_v, params)
    y_sample, s_conv, s_gdn, s_k, s_v, s_logf = trunk(
        x_sample, state_conv, state_gdn, cache_k, cache_v, cache_logf, cache_mem_k, cache_mem_v, params)
    return (y_prompt, y_sample, p_gdn, p_conv, p_k, p_v, p_logf, p_mem_k, p_mem_v,
            s_gdn, s_conv, s_k, s_v, s_logf)
```

```python
import functools

import jax
import jax.numpy as jnp
from jax import lax
from jax.experimental import pallas as pl
from jax.experimental.pallas import tpu as pltpu

F32 = jnp.float32
BF16 = jnp.bfloat16
EPS = 1e-6
HEAD_DIM = 128
LANES = 128
SUBLANES = 8
SCALE = HEAD_DIM ** -0.5
CONV_W = 4
GDN_CHUNK = 64
SOLVE_BLOCK = 16
HI = lax.Precision.HIGHEST
VMEM_LIMIT = 56 * 1024 * 1024


def _params(sem, vmem=VMEM_LIMIT):
    return pltpu.CompilerParams(dimension_semantics=sem, vmem_limit_bytes=vmem)


def _rms(x):
    return x * lax.rsqrt(jnp.mean(x * x, axis=-1, keepdims=True) + EPS)


def _sigmoid(x):
    return 1.0 / (1.0 + jnp.exp(-x))


def _softplus(x):
    return jnp.maximum(x, 0.0) + jnp.log1p(jnp.exp(-jnp.abs(x)))


def _dot(a, b, precision=None):
    return jnp.dot(a, b, preferred_element_type=F32, precision=precision)


def _dot_tn(a, b, precision=None):
    return lax.dot_general(a, b, (((0,), (0,)), ((), ())), preferred_element_type=F32, precision=precision)


def _dot_nt(a, b, precision=None):
    return lax.dot_general(a, b, (((1,), (1,)), ((), ())), preferred_element_type=F32, precision=precision)


def _ep_none(acc, p):
    return acc


def _ep_gdn_gates(acc, p):
    g = -jnp.exp(p[0:1]) * _softplus(acc + p[1:2])
    return jnp.where(p[2:3] > 0.5, g, _sigmoid(acc))


def _ep_log_forget(acc, p):
    return -_softplus(-(acc + p[0:1]))


def _norm_matmul_kernel(x_ref, g_ref, w_ref, p_ref, o_ref, h_ref, *, epilogue):
    @pl.when(pl.program_id(1) == 0)
    def _():
        h_ref[...] = (_rms(x_ref[...]) * g_ref[...]).astype(BF16)

    acc = _dot(h_ref[...], w_ref[...])
    o_ref[...] = epilogue(acc, p_ref[...]).astype(o_ref.dtype)


def norm_matmul(x, g, w, *, tm, tn, out_dtype=F32, epilogue=_ep_none, p=None):
    M, K = x.shape
    N = w.shape[1]
    if p is None:
        p = jnp.zeros((SUBLANES, tn), F32)
    return pl.pallas_call(
        functools.partial(_norm_matmul_kernel, epilogue=epilogue),
        out_shape=jax.ShapeDtypeStruct((M, N), out_dtype),
        grid=(M // tm, N // tn),
        in_specs=[pl.BlockSpec((tm, K), lambda i, j: (i, 0)),
                  pl.BlockSpec((1, K), lambda i, j: (0, 0)),
                  pl.BlockSpec((K, tn), lambda i, j: (0, j)),
                  pl.BlockSpec((SUBLANES, tn), lambda i, j: (0, 0))],
        out_specs=pl.BlockSpec((tm, tn), lambda i, j: (i, j)),
        scratch_shapes=[pltpu.VMEM((tm, K), BF16)],
        compiler_params=_params(("parallel", "arbitrary")),
        name="norm_matmul",
    )(x, g.reshape(1, K), w, p)


def _out_proj_kernel(tok_ref, mem_ref, x_ref, wa_ref, wb_ref, g_ref, o_ref):
    mix = _dot(tok_ref[...], wa_ref[...]) + _dot(mem_ref[...], wb_ref[...])
    o_ref[...] = x_ref[...] + _rms(mix) * g_ref[...]


def out_proj(tok, mem, x, wa, wb, g, *, tm):
    M, D = x.shape
    Ka, Kb = tok.shape[1], mem.shape[1]
    return pl.pallas_call(
        _out_proj_kernel,
        out_shape=jax.ShapeDtypeStruct((M, D), F32),
        grid=(M // tm,),
        in_specs=[pl.BlockSpec((tm, Ka), lambda i: (i, 0)),
                  pl.BlockSpec((tm, Kb), lambda i: (i, 0)),
                  pl.BlockSpec((tm, D), lambda i: (i, 0)),
                  pl.BlockSpec((Ka, D), lambda i: (0, 0)),
                  pl.BlockSpec((Kb, D), lambda i: (0, 0)),
                  pl.BlockSpec((1, D), lambda i: (0, 0))],
        out_specs=pl.BlockSpec((tm, D), lambda i: (i, 0)),
        compiler_params=_params(("parallel",)),
        name="out_proj",
    )(tok, mem, x, wa, wb, g.reshape(1, D))


def _mlp_kernel(x_ref, gpre_ref, gpost_ref, wup_ref, wdn_ref, o_ref, h_ref, acc_ref):
    j = pl.program_id(1)

    @pl.when(j == 0)
    def _():
        h_ref[...] = (_rms(x_ref[...]) * gpre_ref[...]).astype(BF16)

    up = _dot(h_ref[...], wup_ref[...])
    act = jnp.square(jnp.maximum(up, 0.0)).astype(BF16)
    part = _dot(act, wdn_ref[...])

    @pl.when(j == 0)
    def _():
        acc_ref[...] = part

    @pl.when(j > 0)
    def _():
        acc_ref[...] += part

    @pl.when(j == pl.num_programs(1) - 1)
    def _():
        o_ref[...] = x_ref[...] + _rms(acc_ref[...]) * gpost_ref[...]


def mlp(x, gpre, gpost, wup, wdn, *, tm, tf):
    M, D = x.shape
    FF = wup.shape[1]
    return pl.pallas_call(
        _mlp_kernel,
        out_shape=jax.ShapeDtypeStruct((M, D), F32),
        grid=(M // tm, FF // tf),
        in_specs=[pl.BlockSpec((tm, D), lambda i, j: (i, 0)),
                  pl.BlockSpec((1, D), lambda i, j: (0, 0)),
                  pl.BlockSpec((1, D), lambda i, j: (0, 0)),
                  pl.BlockSpec((D, tf), lambda i, j: (0, j)),
                  pl.BlockSpec((tf, D), lambda i, j: (j, 0))],
        out_specs=pl.BlockSpec((tm, D), lambda i, j: (i, 0)),
        scratch_shapes=[pltpu.VMEM((tm, D), BF16), pltpu.VMEM((tm, D), F32)],
        compiler_params=_params(("parallel", "arbitrary")),
        name="mlp",
    )(x, gpre.reshape(1, D), gpost.reshape(1, D), wup, wdn)


def _mem_attn_kernel(q_ref, mk_ref, mv_ref, o_ref, *, heads):
    for h in range(heads):
        sl = slice(h * HEAD_DIM, (h + 1) * HEAD_DIM)
        q = q_ref[:, sl].astype(BF16)
        k = mk_ref[0, :, sl].astype(BF16)
        v = mv_ref[0, :, sl].astype(BF16)
        s = _dot_nt(q, k) * SCALE
        e = jnp.exp(s - jnp.max(s, axis=-1, keepdims=True))
        p = e / jnp.sum(e, axis=-1, keepdims=True)
        o_ref[:, sl] = _dot(p.astype(BF16), v).astype(o_ref.dtype)


def mem_attend(proj, mk, mv, *, row0, rows_per_seq, tm, col_block):
    nseq, nmem, W = mk.shape
    nt = rows_per_seq // tm
    rb0 = row0 // tm
    return pl.pallas_call(
        functools.partial(_mem_attn_kernel, heads=W // HEAD_DIM),
        out_shape=jax.ShapeDtypeStruct((nseq * rows_per_seq, W), BF16),
        grid=(nseq, nt),
        in_specs=[pl.BlockSpec((tm, W), lambda b, i: (rb0 + b * nt + i, col_block)),
                  pl.BlockSpec((1, nmem, W), lambda b, i: (b, 0, 0)),
                  pl.BlockSpec((1, nmem, W), lambda b, i: (b, 0, 0))],
        out_specs=pl.BlockSpec((tm, W), lambda b, i: (b * nt + i, 0)),
        compiler_params=_params(("parallel", "parallel")),
        name="mem_attend",
    )(proj, mk, mv)


def _unit_lower_solve(A, rhs, L):
    row = lax.broadcasted_iota(jnp.int32, (L, L), 0)
    col = lax.broadcasted_iota(jnp.int32, (L, L), 1)
    same = (row // SOLVE_BLOCK) == (col // SOLVE_BLOCK)
    eye = (row == col).astype(F32)
    x = jnp.where(same, -A, 0.0)
    a_off = jnp.where(same, 0.0, A)
    td = eye + x
    p = x
    span = 2
    while span < SOLVE_BLOCK:
        p = _dot(p, p, HI)
        td = td + _dot(td, p, HI)
        span *= 2
    y = _dot(td, rhs, HI)
    nblocks = L // SOLVE_BLOCK
    if nblocks > 1:
        n = _dot(td, a_off, HI)
        powers = [n]
        span = 2
        while span < nblocks:
            powers.append(_dot(powers[-1], powers[-1], HI))
            span *= 2
        for pw in reversed(powers[1:]):
            y = y + _dot(pw, y, HI)
        y = y - _dot(n, y, HI)
    return y


def _gdn_kernel(q_ref, k_ref, v_ref, z_ref, gb_ref, grow_ref, hq_ref, hk_ref, hv_ref,
                cwq_ref, cwk_ref, cwv_ref, s0_ref, gn_ref,
                o_ref, sfin_ref, s_scr, prev_scr, *, L, Hb, H, prec):
    hg = pl.program_id(1)
    c = pl.program_id(2)

    @pl.when(c == 0)
    def _():
        s_scr[...] = s0_ref[0]
        prev_scr[...] = jnp.zeros_like(prev_scr)
        for t, h_ref in enumerate((hq_ref, hk_ref, hv_ref)):
            prev_scr[t, L - SUBLANES:L, :] = h_ref[0]

    rowi = lax.broadcasted_iota(jnp.int32, (L, 1), 0)

    def conv_silu(t, x_ref, cw_ref):
        x = x_ref[...]
        prev = prev_scr[t]
        w = cw_ref[...]
        y = None
        for s in range(CONV_W - 1, 0, -1):
            sh = jnp.where(rowi >= s, pltpu.roll(x, s, 0), pltpu.roll(prev, s, 0))
            term = sh * w[CONV_W - 1 - s:CONV_W - s]
            y = term if y is None else y + term
        y = y + x * w[CONV_W - 1:CONV_W]
        prev_scr[t] = x
        return y * _sigmoid(y)

    yq = conv_silu(0, q_ref, cwq_ref)
    yk = conv_silu(1, k_ref, cwk_ref)
    yv = conv_silu(2, v_ref, cwv_ref)

    r2 = lax.broadcasted_iota(jnp.int32, (L, L), 0)
    c2 = lax.broadcasted_iota(jnp.int32, (L, L), 1)
    incl = r2 >= c2
    strict = r2 > c2
    tril = incl.astype(F32)
    triu = (r2 <= c2).astype(F32)

    gb = gb_ref[...]
    gcum = _dot(tril, gb, HI)
    grow_cum = _dot(grow_ref[0, 0], triu, HI)
    lane = lax.broadcasted_iota(jnp.int32, (L, LANES), 1)
    gnorm = gn_ref[...]

    def lane_col(tile, idx):
        return jnp.sum(jnp.where(lane == idx, tile, 0.0), axis=1, keepdims=True)

    for i in range(Hb):
        head = hg * Hb + i
        sl = slice(i * HEAD_DIM, (i + 1) * HEAD_DIM)
        G = lane_col(gcum, head)
        beta = lane_col(gb, head + H)
        Gr = grow_cum[i:i + 1, :]
        GL = Gr[:, L - 1:L]
        q = yq[:, sl]
        k = yk[:, sl]
        v = yv[:, sl]
        q = q * lax.rsqrt(jnp.sum(q * q, axis=-1, keepdims=True) + EPS) * SCALE
        k = k * lax.rsqrt(jnp.sum(k * k, axis=-1, keepdims=True) + EPS)
        eG = jnp.exp(G)
        decay = jnp.exp(jnp.where(incl, G - Gr, -jnp.inf))
        kk = _dot_nt(k, k, prec)
        qk = _dot_nt(q, k, prec)
        A = jnp.where(strict, beta * kk * decay, 0.0)
        rhs = jnp.concatenate([beta * v, (beta * eG) * k], axis=-1)
        sol = _unit_lower_solve(A, rhs, L)
        u = sol[:, :HEAD_DIM]
        w = sol[:, HEAD_DIM:]
        S = s_scr[i]
        v_new = u - _dot(w, S, prec)
        out = _dot(q * eG, S, prec) + _dot(qk * decay, v_new, prec)
        kd = k * jnp.exp(GL - G)
        s_scr[i] = jnp.exp(GL) * S + _dot_tn(kd, v_new, prec)
        zz = z_ref[:, sl]
        o_ref[:, sl] = (_rms(out) * gnorm * (zz * _sigmoid(zz))).astype(o_ref.dtype)

    @pl.when(c == pl.num_programs(2) - 1)
    def _():
        sfin_ref[0] = s_scr[...]


def gdn_mix(proj, gb, hist8, cw8, s0, gnorm, *, row0, nseq, T, L, Hb, H, prec=HI):
    NC = T // L
    HG = H // Hb
    W = Hb * HEAD_DIM
    rb0 = row0 // L
    rows = nseq * T
    g_rows = gb[row0:row0 + rows, :H].reshape(nseq * NC, L, HG, Hb).transpose(0, 2, 3, 1)
    R = max(Hb, SUBLANES)
    g_rows = jnp.pad(g_rows, ((0, 0), (0, 0), (0, R - Hb), (0, 0)))

    def col(group):
        return lambda b, g, c: (rb0 + b * NC + c, group * HG + g)

    def hist(group):
        return lambda b, g, c: (b, 0, group * HG + g)

    def cwm(group):
        return lambda b, g, c: (0, group * HG + g)

    return pl.pallas_call(
        functools.partial(_gdn_kernel, L=L, Hb=Hb, H=H, prec=prec),
        out_shape=(jax.ShapeDtypeStruct((rows, H * HEAD_DIM), BF16),
                   jax.ShapeDtypeStruct((nseq, H, HEAD_DIM, HEAD_DIM), F32)),
        grid=(nseq, HG, NC),
        in_specs=[pl.BlockSpec((L, W), col(0)), pl.BlockSpec((L, W), col(1)),
                  pl.BlockSpec((L, W), col(2)), pl.BlockSpec((L, W), col(3)),
                  pl.BlockSpec((L, LANES), lambda b, g, c: (rb0 + b * NC + c, 0)),
                  pl.BlockSpec((1, 1, R, L), lambda b, g, c: (b * NC + c, g, 0, 0)),
                  pl.BlockSpec((1, SUBLANES, W), hist(0)), pl.BlockSpec((1, SUBLANES, W), hist(1)),
                  pl.BlockSpec((1, SUBLANES, W), hist(2)),
                  pl.BlockSpec((SUBLANES, W), cwm(0)), pl.BlockSpec((SUBLANES, W), cwm(1)),
                  pl.BlockSpec((SUBLANES, W), cwm(2)),
                  pl.BlockSpec((1, Hb, HEAD_DIM, HEAD_DIM), lambda b, g, c: (b, g, 0, 0)),
                  pl.BlockSpec((1, HEAD_DIM), lambda b, g, c: (0, 0))],
        out_specs=(pl.BlockSpec((L, W), lambda b, g, c: (b * NC + c, g)),
                   pl.BlockSpec((1, Hb, HEAD_DIM, HEAD_DIM), lambda b, g, c: (b, g, 0, 0))),
        scratch_shapes=[pltpu.VMEM((Hb, HEAD_DIM, HEAD_DIM), F32),
                        pltpu.VMEM((3, L, W), F32)],
        compiler_params=_params(("parallel", "parallel", "arbitrary")),
        name="gdn_mix",
    )(proj, proj, proj, proj, gb, g_rows, hist8, hist8, hist8, cw8, cw8, cw8, s0, gnorm.reshape(1, HEAD_DIM))


def _cumsum_kernel(x_ref, o_ref, carry_ref, *, tb):
    @pl.when(pl.program_id(1) == 0)
    def _():
        carry_ref[...] = jnp.zeros_like(carry_ref)

    r = lax.broadcasted_iota(jnp.int32, (tb, tb), 0)
    c = lax.broadcasted_iota(jnp.int32, (tb, tb), 1)
    out = _dot(x_ref[0], (r <= c).astype(F32), HI) + carry_ref[...]
    o_ref[0] = out
    carry_ref[...] = out[:, tb - 1:tb]


def cumsum_lanes(x, *, tb):
    n, H, T = x.shape
    return pl.pallas_call(
        functools.partial(_cumsum_kernel, tb=tb),
        out_shape=jax.ShapeDtypeStruct((n, H, T), F32),
        grid=(n, T // tb),
        in_specs=[pl.BlockSpec((1, H, tb), lambda b, i: (b, 0, i))],
        out_specs=pl.BlockSpec((1, H, tb), lambda b, i: (b, 0, i)),
        scratch_shapes=[pltpu.VMEM((H, 1), F32)],
        compiler_params=_params(("parallel", "arbitrary")),
        name="cumsum_lanes",
    )(x)


def _fox_prompt_kernel(q_ref, gate_ref, k_ref, v_ref, cq_ref, ck_ref, o_ref,
                       qb_scr, cq_scr, m_scr, l_scr, acc_scr, *, tq, tk):
    h = pl.program_id(1)
    qi = pl.program_id(2)
    ki = pl.program_id(3)

    @pl.when(ki == 0)
    def _():
        qb_scr[...] = q_ref[...].astype(BF16)
        lane = lax.broadcasted_iota(jnp.int32, cq_ref.shape, 1)
        cq_scr[...] = jnp.sum(jnp.where(lane == h, cq_ref[...], 0.0), axis=1, keepdims=True)
        m_scr[...] = jnp.full_like(m_scr, -jnp.inf)
        l_scr[...] = jnp.zeros_like(l_scr)
        acc_scr[...] = jnp.zeros_like(acc_scr)

    last = ((qi + 1) * tq - 1) // tk

    @pl.when(ki <= last)
    def _():
        s = _dot_nt(qb_scr[...], k_ref[...].astype(BF16)) * SCALE
        s = s + cq_scr[...] - ck_ref[0]
        qpos = qi * tq + lax.broadcasted_iota(jnp.int32, (tq, tk), 0)
        kpos = ki * tk + lax.broadcasted_iota(jnp.int32, (tq, tk), 1)
        s = jnp.where(kpos <= qpos, s, -jnp.inf)
        m_new = jnp.maximum(m_scr[...], jnp.max(s, axis=-1, keepdims=True))
        alpha = jnp.exp(m_scr[...] - m_new)
        p = jnp.exp(s - m_new)
        l_scr[...] = alpha * l_scr[...] + jnp.sum(p, axis=-1, keepdims=True)
        acc_scr[...] = alpha * acc_scr[...] + _dot(p.astype(BF16), v_ref[...].astype(BF16))
        m_scr[...] = m_new

    @pl.when(ki == last)
    def _():
        o_ref[...] = (acc_scr[...] / l_scr[...] * _sigmoid(gate_ref[...])).astype(o_ref.dtype)


def fox_prompt(proj, karr, varr, c_col, c_rows, *, nseq, T, H, tq, tk):
    nq, nk = T // tq, T // tk

    def kv_map(b, h, qi, ki):
        return (b * nk + jnp.minimum(ki, ((qi + 1) * tq - 1) // tk), h)

    return pl.pallas_call(
        functools.partial(_fox_prompt_kernel, tq=tq, tk=tk),
        out_shape=jax.ShapeDtypeStruct((nseq * T, H * HEAD_DIM), BF16),
        grid=(nseq, H, nq, nk),
        in_specs=[pl.BlockSpec((tq, HEAD_DIM), lambda b, h, qi, ki: (b * nq + qi, h)),
                  pl.BlockSpec((tq, HEAD_DIM), lambda b, h, qi, ki: (b * nq + qi, H + h)),
                  pl.BlockSpec((tk, HEAD_DIM), kv_map),
                  pl.BlockSpec((tk, HEAD_DIM), kv_map),
                  pl.BlockSpec((tq, H), lambda b, h, qi, ki: (b * nq + qi, 0)),
                  pl.BlockSpec((1, 1, tk),
                               lambda b, h, qi, ki: (b * H + h, 0, jnp.minimum(ki, ((qi + 1) * tq - 1) // tk)))],
        out_specs=pl.BlockSpec((tq, HEAD_DIM), lambda b, h, qi, ki: (b * nq + qi, h)),
        scratch_shapes=[pltpu.VMEM((tq, HEAD_DIM), BF16), pltpu.VMEM((tq, 1), F32),
                        pltpu.VMEM((tq, 1), F32), pltpu.VMEM((tq, 1), F32),
                        pltpu.VMEM((tq, HEAD_DIM), F32)],
        compiler_params=_params(("parallel", "parallel", "parallel", "arbitrary")),
        name="fox_prompt",
    )(proj, proj, karr, varr, c_col, c_rows)


def _fox_sample_kernel(q_ref, gate_ref, ck_ref, cv_ref, kn_ref, vn_ref, cq_ref, cc_ref, cn_ref, o_ref,
                       m_scr, l_scr, acc_scr, *, H, T):
    ki = pl.program_id(1)

    @pl.when(ki == 0)
    def _():
        m_scr[...] = jnp.full_like(m_scr, -jnp.inf)
        l_scr[...] = jnp.zeros_like(l_scr)
        acc_scr[...] = jnp.zeros_like(acc_scr)

    cq = cq_ref[...]

    def update(h, k, v, ck_row, mask):
        sl = slice(h * HEAD_DIM, (h + 1) * HEAD_DIM)
        q = q_ref[:, sl].astype(BF16)
        s = _dot_nt(q, k.astype(BF16)) * SCALE
        s = s + cq[:, h:h + 1] - ck_row
        if mask is not None:
            s = jnp.where(mask, s, -jnp.inf)
        m_old = m_scr[h]
        m_new = jnp.maximum(m_old, jnp.max(s, axis=-1, keepdims=True))
        alpha = jnp.exp(m_old - m_new)
        p = jnp.exp(s - m_new)
        l_scr[h] = alpha * l_scr[h] + jnp.sum(p, axis=-1, keepdims=True)
        acc_scr[:, sl] = alpha * acc_scr[:, sl] + _dot(p.astype(BF16), v.astype(BF16))
        m_scr[h] = m_new

    for h in range(H):
        sl = slice(h * HEAD_DIM, (h + 1) * HEAD_DIM)
        update(h, ck_ref[0, :, sl], cv_ref[0, :, sl], cc_ref[0, h:h + 1, :], None)

    @pl.when(ki == pl.num_programs(1) - 1)
    def _():
        causal = (lax.broadcasted_iota(jnp.int32, (T, T), 1) <= lax.broadcasted_iota(jnp.int32, (T, T), 0))
        for h in range(H):
            sl = slice(h * HEAD_DIM, (h + 1) * HEAD_DIM)
            update(h, kn_ref[:, sl], vn_ref[:, sl], cn_ref[0, h:h + 1, :], causal)
            o_ref[:, sl] = (acc_scr[:, sl] / l_scr[h] * _sigmoid(gate_ref[:, sl])).astype(o_ref.dtype)


def fox_sample(proj, cache_k, cache_v, karr, varr, c_col, c_cache, c_new, *, row0, nseq, T, H, tk):
    P = cache_k.shape[1]
    W = H * HEAD_DIM
    rb0 = row0 // T
    return pl.pallas_call(
        functools.partial(_fox_sample_kernel, H=H, T=T),
        out_shape=jax.ShapeDtypeStruct((nseq * T, W), BF16),
        grid=(nseq, P // tk),
        in_specs=[pl.BlockSpec((T, W), lambda b, i: (rb0 + b, 0)),
                  pl.BlockSpec((T, W), lambda b, i: (rb0 + b, 1)),
                  pl.BlockSpec((1, tk, W), lambda b, i: (b, i, 0)),
                  pl.BlockSpec((1, tk, W), lambda b, i: (b, i, 0)),
                  pl.BlockSpec((T, W), lambda b, i: (rb0 + b, 0)),
                  pl.BlockSpec((T, W), lambda b, i: (rb0 + b, 0)),
                  pl.BlockSpec((T, H), lambda b, i: (b, 0)),
                  pl.BlockSpec((1, H, tk), lambda b, i: (b, 0, i)),
                  pl.BlockSpec((1, H, T), lambda b, i: (b, 0, 0))],
        out_specs=pl.BlockSpec((T, W), lambda b, i: (b, 0)),
        scratch_shapes=[pltpu.VMEM((H, T, 1), F32), pltpu.VMEM((H, T, 1), F32), pltpu.VMEM((T, W), F32)],
        compiler_params=_params(("parallel", "arbitrary")),
        name="fox_sample",
    )(proj, proj, cache_k, cache_v, karr, varr, c_col, c_cache, c_new)


def _pad_cols(a, width):
    return jnp.pad(a, ((0, 0), (0, width - a.shape[1])))


def kernel(x_prompt, x_sample, state_gdn, state_conv, cache_k, cache_v, cache_logf, cache_mem_k, cache_mem_v, mem_prompt, norm_mix_pre, norm_mix_post, norm_mlp_pre, norm_mlp_post, w_in_a, conv_w_a, a_log, dt_bias, gdn_norm, w_in_b, norm_kv, w_kvf, b_f, norm_mem, w_mem_kv, w_o, w_up, w_down):
    Bp, Tp, D = x_prompt.shape
    Bs, Ts, _ = x_sample.shape
    n_a = w_in_a.shape[0]
    depth = w_o.shape[0]
    H = a_log.shape[1]
    tokd = H * HEAD_DIM
    qkvd = 3 * tokd
    P = cache_k.shape[1]
    nmem = mem_prompt.shape[1]
    memd = cache_mem_k.shape[3] * cache_mem_k.shape[4]
    Mp, Ms = Bp * Tp, Bs * Ts
    M = Mp + Ms
    TM = next(t for t in (1024, 512, 256, 128, 64, 32) if M % t == 0)
    TMH = max(TM // 2, 32)

    x = jnp.concatenate([x_prompt.reshape(Mp, D), x_sample.reshape(Ms, D)], axis=0)

    mem_rows = mem_prompt.reshape(Bp * nmem, D)
    pmk, pmv = [], []
    for l in range(depth):
        wkv = w_mem_kv[l].astype(BF16)
        pmk.append(norm_matmul(mem_rows, norm_mem[l], wkv[:, :memd], tm=min(TM, Bp * nmem), tn=memd))
        pmv.append(norm_matmul(mem_rows, norm_mem[l], wkv[:, memd:], tm=min(TM, Bp * nmem), tn=memd))
    p_mem_k = jnp.stack(pmk).reshape(depth, Bp, nmem, memd)
    p_mem_v = jnp.stack(pmv).reshape(depth, Bp, nmem, memd)
    s_mem_k = cache_mem_k.reshape(depth, Bs, nmem, memd)
    s_mem_v = cache_mem_v.reshape(depth, Bs, nmem, memd)

    def hist8(h):
        return jnp.pad(h, ((0, 0), (SUBLANES - (CONV_W - 1), 0), (0, 0)))

    p_gdn, s_gdn, p_conv, s_conv = [], [], [], []
    lane = jnp.arange(LANES)
    k_arr = v_arr = None
    for l in range(depth):
        if l == n_a:
            wk = w_kvf[:, :tokd].astype(BF16)
            wv = w_kvf[:, tokd:2 * tokd].astype(BF16)
            wf = _pad_cols(w_kvf[:, 2 * tokd:], LANES).astype(BF16)
            k_arr = norm_matmul(x, norm_kv, wk, tm=TM, tn=512)
            v_arr = norm_matmul(x, norm_kv, wv, tm=TM, tn=512)
            pf = jnp.zeros((SUBLANES, LANES), F32).at[0, :H].set(b_f)
            logf = norm_matmul(x, norm_kv, wf, tm=TM, tn=LANES, epilogue=_ep_log_forget, p=pf)[:, :H]
            lf_p = logf[:Mp].reshape(Bp, Tp, H)
            lf_s = logf[Mp:].reshape(Bs, Ts, H)
            cp_rows = cumsum_lanes(lf_p.transpose(0, 2, 1), tb=min(Tp, 512))
            cp_col = cp_rows.transpose(0, 2, 1).reshape(Mp, H)
            tot = P + Ts
            tot_pad = -(-tot // LANES) * LANES
            lf_all = jnp.concatenate([cache_logf, lf_s], axis=1).transpose(0, 2, 1)
            lf_all = jnp.pad(lf_all, ((0, 0), (0, 0), (0, tot_pad - tot)))
            cs_rows = cumsum_lanes(lf_all, tb=LANES)
            cs_cache = cs_rows[:, :, :P]
            cs_new = cs_rows[:, :, P:tot]
            cs_col = cs_new.transpose(0, 2, 1).reshape(Ms, H)

        wo = w_o[l].astype(BF16)
        if l < n_a:
            w = w_in_a[l]
            o1 = qkvd + tokd
            w_main = jnp.concatenate([w[:, :o1], w[:, o1 + 2 * H:]], axis=1).astype(BF16)
            w_ab = _pad_cols(w[:, o1:o1 + 2 * H], LANES).astype(BF16)
            proj = norm_matmul(x, norm_mix_pre[l], w_main, tm=TM, tn=512)
            pg = jnp.zeros((SUBLANES, LANES), F32)
            pg = pg.at[0, :H].set(a_log[l]).at[1, :H].set(dt_bias[l]).at[2].set((lane < H).astype(F32))
            gb = norm_matmul(x, norm_mix_pre[l], w_ab, tm=TM, tn=LANES, epilogue=_ep_gdn_gates, p=pg)
            cw8 = jnp.pad(conv_w_a[l], ((0, SUBLANES - CONV_W), (0, 0)))
            tok_p, sp = gdn_mix(proj, gb, hist8(jnp.zeros((Bp, CONV_W - 1, qkvd), F32)), cw8,
                                jnp.zeros((Bp, H, HEAD_DIM, HEAD_DIM), F32), gdn_norm[l],
                                row0=0, nseq=Bp, T=Tp, L=min(Tp, GDN_CHUNK), Hb=4, H=H)
            tok_s, ss = gdn_mix(proj, gb, hist8(state_conv[l]), cw8, state_gdn[l], gdn_norm[l],
                                row0=Mp, nseq=Bs, T=Ts, L=min(Ts, GDN_CHUNK), Hb=4, H=H)
            p_gdn.append(sp)
            s_gdn.append(ss)
            p_conv.append(proj[:Mp].reshape(Bp, Tp, -1)[:, Tp - (CONV_W - 1):, :qkvd])
            s_conv.append(proj[Mp:].reshape(Bs, Ts, -1)[:, Ts - (CONV_W - 1):, :qkvd])
            mq_block = (qkvd + tokd) // memd
        else:
            proj = norm_matmul(x, norm_mix_pre[l], w_in_b[l - n_a].astype(BF16), tm=TM, tn=512)
            tok_p = fox_prompt(proj, k_arr, v_arr, cp_col, cp_rows.reshape(Bp * H, 1, Tp),
                               nseq=Bp, T=Tp, H=H, tq=min(Tp, 512), tk=min(Tp, 512))
            tok_s = fox_sample(proj, cache_k.reshape(Bs, P, tokd), cache_v.reshape(Bs, P, tokd), k_arr, v_arr,
                               cs_col, cs_cache, cs_new, row0=Mp, nseq=Bs, T=Ts, H=H, tk=min(P, 512))
            mq_block = 2 * tokd // memd
        tok = jnp.concatenate([tok_p, tok_s], axis=0)
        mem_p = mem_attend(proj, p_mem_k[l], p_mem_v[l], row0=0, rows_per_seq=Tp, tm=min(Tp, 512),
                           col_block=mq_block)
        mem_s = mem_attend(proj, s_mem_k[l], s_mem_v[l], row0=Mp, rows_per_seq=Ts, tm=Ts, col_block=mq_block)
        mem_o = jnp.concatenate([mem_p, mem_s], axis=0)
        x = out_proj(tok, mem_o, x, wo[:tokd], wo[tokd:], norm_mix_post[l], tm=TMH)
        x = mlp(x, norm_mlp_pre[l], norm_mlp_post[l], w_up[l].astype(BF16), w_down[l].astype(BF16),
                tm=TMH, tf=512)

    mh = cache_mem_k.shape[3]
    y_prompt = x[:Mp].reshape(Bp, Tp, D)
    y_sample = x[Mp:].reshape(Bs, Ts, D)
    return (y_prompt, y_sample, jnp.stack(p_gdn), jnp.stack(p_conv),
            k_arr[:Mp].reshape(Bp, Tp, H, HEAD_DIM), v_arr[:Mp].reshape(Bp, Tp, H, HEAD_DIM), lf_p,
            p_mem_k.reshape(depth, Bp, nmem, mh, HEAD_DIM), p_mem_v.reshape(depth, Bp, nmem, mh, HEAD_DIM),
            jnp.stack(s_gdn), jnp.stack(s_conv),
            k_arr[Mp:].reshape(Bs, Ts, H, HEAD_DIM), v_arr[Mp:].reshape(Bs, Ts, H, HEAD_DIM), lf_s)
```

```python
import functools
import math

import jax
import jax.numpy as jnp
from jax import lax
from jax.experimental import pallas as pl
from jax.experimental.pallas import tpu as pltpu

F32 = jnp.float32
BF16 = jnp.bfloat16
EPS = 1e-6
HEAD_DIM = 128
LANES = 128
SUBLANES = 8
SCALE = HEAD_DIM ** -0.5
LOG2E = math.log2(math.e)
CONV_W = 4
GDN_CHUNK = 64
GDN_STACK = 4
SOLVE_BLOCK = 16
HI = lax.Precision.HIGHEST
VMEM_LIMIT = 56 * 1024 * 1024


def _params(sem, vmem=VMEM_LIMIT):
    return pltpu.CompilerParams(dimension_semantics=sem, vmem_limit_bytes=vmem)


def _rms(x):
    return x * lax.rsqrt(jnp.mean(x * x, axis=-1, keepdims=True) + EPS)


def _sigmoid(x):
    return 1.0 / (1.0 + jnp.exp(-x))


def _softplus(x):
    return jnp.maximum(x, 0.0) + jnp.log1p(jnp.exp(-jnp.abs(x)))


def _dot(a, b, precision=None):
    return jnp.dot(a, b, preferred_element_type=F32, precision=precision)


def _dot_tn(a, b):
    return lax.dot_general(a, b, (((0,), (0,)), ((), ())), preferred_element_type=F32)


def _dot_nt(a, b):
    return lax.dot_general(a, b, (((1,), (1,)), ((), ())), preferred_element_type=F32)


def _split2(x):
    hi = x.astype(BF16)
    return hi, (x - hi.astype(F32)).astype(BF16)


def _split3(x):
    hi = x.astype(BF16)
    r = x - hi.astype(F32)
    mid = r.astype(BF16)
    return hi, mid, (r - mid.astype(F32)).astype(BF16)


def _dot3(a, b):
    return _dot(a[0], b[0]) + (_dot(a[0], b[1]) + _dot(a[1], b[0]))


def _ep_none(acc, p):
    return acc


def _ep_gdn_gates(acc, p):
    g = -jnp.exp(p[0:1]) * _softplus(acc + p[1:2])
    return jnp.where(p[2:3] > 0.5, g, _sigmoid(acc))


def _ep_log_forget(acc, p):
    return -_softplus(-(acc + p[0:1]))


def _norm_matmul_kernel(x_ref, g_ref, w_ref, p_ref, o_ref, h_ref, *, epilogue):
    @pl.when(pl.program_id(1) == 0)
    def _():
        h_ref[...] = (_rms(x_ref[...]) * g_ref[...]).astype(BF16)

    acc = _dot(h_ref[...], w_ref[...])
    o_ref[...] = epilogue(acc, p_ref[...]).astype(o_ref.dtype)


def norm_matmul(x, g, w, *, tm, tn, out_dtype=F32, epilogue=_ep_none, p=None):
    M, K = x.shape
    N = w.shape[1]
    if p is None:
        p = jnp.zeros((SUBLANES, tn), F32)
    return pl.pallas_call(
        functools.partial(_norm_matmul_kernel, epilogue=epilogue),
        out_shape=jax.ShapeDtypeStruct((M, N), out_dtype),
        grid=(M // tm, N // tn),
        in_specs=[pl.BlockSpec((tm, K), lambda i, j: (i, 0)),
                  pl.BlockSpec((1, K), lambda i, j: (0, 0)),
                  pl.BlockSpec((K, tn), lambda i, j: (0, j)),
                  pl.BlockSpec((SUBLANES, tn), lambda i, j: (0, 0))],
        out_specs=pl.BlockSpec((tm, tn), lambda i, j: (i, j)),
        scratch_shapes=[pltpu.VMEM((tm, K), BF16)],
        compiler_params=_params(("parallel", "arbitrary")),
        name="norm_matmul",
    )(x, g.reshape(1, K), w, p)


def _kv_proj_kernel(x_ref, g_ref, w_ref, o2_ref, o4_ref, *, H):
    h = (_rms(x_ref[...]) * g_ref[...]).astype(BF16)
    acc = _dot(h, w_ref[...])
    o2_ref[...] = acc.astype(o2_ref.dtype)
    for hh in range(H):
        o4_ref[:, hh, :] = acc[:, hh * HEAD_DIM:(hh + 1) * HEAD_DIM]


def kv_proj(x, g, w, *, row0, rows, tm, H):
    K = x.shape[1]
    N = w.shape[1]
    rb0 = row0 // tm
    return pl.pallas_call(
        functools.partial(_kv_proj_kernel, H=H),
        out_shape=(jax.ShapeDtypeStruct((rows, N), BF16),
                   jax.ShapeDtypeStruct((rows, H, HEAD_DIM), F32)),
        grid=(rows // tm,),
        in_specs=[pl.BlockSpec((tm, K), lambda i: (rb0 + i, 0)),
                  pl.BlockSpec((1, K), lambda i: (0, 0)),
                  pl.BlockSpec((K, N), lambda i: (0, 0))],
        out_specs=(pl.BlockSpec((tm, N), lambda i: (i, 0)),
                   pl.BlockSpec((tm, H, HEAD_DIM), lambda i: (i, 0, 0))),
        compiler_params=_params(("parallel",)),
        name="kv_proj",
    )(x, g.reshape(1, K), w)


def _out_proj_kernel(tok_ref, mem_ref, x_ref, wa_ref, wb_ref, g_ref, o_ref):
    mix = _dot(tok_ref[...], wa_ref[...]) + _dot(mem_ref[...], wb_ref[...])
    o_ref[...] = x_ref[...] + _rms(mix) * g_ref[...]


def out_proj(tok, mem, x, wa, wb, g, *, tm):
    M, D = x.shape
    Ka, Kb = tok.shape[1], mem.shape[1]
    return pl.pallas_call(
        _out_proj_kernel,
        out_shape=jax.ShapeDtypeStruct((M, D), F32),
        grid=(M // tm,),
        in_specs=[pl.BlockSpec((tm, Ka), lambda i: (i, 0)),
                  pl.BlockSpec((tm, Kb), lambda i: (i, 0)),
                  pl.BlockSpec((tm, D), lambda i: (i, 0)),
                  pl.BlockSpec((Ka, D), lambda i: (0, 0)),
                  pl.BlockSpec((Kb, D), lambda i: (0, 0)),
                  pl.BlockSpec((1, D), lambda i: (0, 0))],
        out_specs=pl.BlockSpec((tm, D), lambda i: (i, 0)),
        compiler_params=_params(("parallel",)),
        name="out_proj",
    )(tok, mem, x, wa, wb, g.reshape(1, D))


def _mlp_kernel(x_ref, gpre_ref, gpost_ref, wup_ref, wdn_ref, o_ref, h_ref, acc_ref):
    j = pl.program_id(1)

    @pl.when(j == 0)
    def _():
        h_ref[...] = (_rms(x_ref[...]) * gpre_ref[...]).astype(BF16)

    up = _dot(h_ref[...], wup_ref[...])
    act = jnp.square(jnp.maximum(up, 0.0)).astype(BF16)
    part = _dot(act, wdn_ref[...])

    @pl.when(j == 0)
    def _():
        acc_ref[...] = part

    @pl.when(j > 0)
    def _():
        acc_ref[...] += part

    @pl.when(j == pl.num_programs(1) - 1)
    def _():
        o_ref[...] = x_ref[...] + _rms(acc_ref[...]) * gpost_ref[...]


def mlp(x, gpre, gpost, wup, wdn, *, tm, tf):
    M, D = x.shape
    FF = wup.shape[1]
    return pl.pallas_call(
        _mlp_kernel,
        out_shape=jax.ShapeDtypeStruct((M, D), F32),
        grid=(M // tm, FF // tf),
        in_specs=[pl.BlockSpec((tm, D), lambda i, j: (i, 0)),
                  pl.BlockSpec((1, D), lambda i, j: (0, 0)),
                  pl.BlockSpec((1, D), lambda i, j: (0, 0)),
                  pl.BlockSpec((D, tf), lambda i, j: (0, j)),
                  pl.BlockSpec((tf, D), lambda i, j: (j, 0))],
        out_specs=pl.BlockSpec((tm, D), lambda i, j: (i, 0)),
        scratch_shapes=[pltpu.VMEM((tm, D), BF16), pltpu.VMEM((tm, D), F32)],
        compiler_params=_params(("parallel", "arbitrary")),
        name="mlp",
    )(x, gpre.reshape(1, D), gpost.reshape(1, D), wup, wdn)


def _mem_attn_kernel(q_ref, mk_ref, mv_ref, o_ref, *, heads):
    for h in range(heads):
        sl = slice(h * HEAD_DIM, (h + 1) * HEAD_DIM)
        q = q_ref[:, sl].astype(BF16)
        k = mk_ref[0, 0, :, h, :].astype(BF16)
        v = mv_ref[0, 0, :, h, :].astype(BF16)
        s = _dot_nt(q, k) * SCALE
        e = jnp.exp(s - jnp.max(s, axis=-1, keepdims=True))
        p = e / jnp.sum(e, axis=-1, keepdims=True)
        o_ref[:, sl] = _dot(p.astype(BF16), v).astype(o_ref.dtype)


def mem_attend(proj, mk, mv, layer, *, row0, rows_per_seq, tm, col_block):
    _, nseq, nmem, heads, _ = mk.shape
    W = heads * HEAD_DIM
    nt = rows_per_seq // tm
    rb0 = row0 // tm
    mem_spec = pl.BlockSpec((1, 1, nmem, heads, HEAD_DIM), lambda b, i: (layer, b, 0, 0, 0))
    return pl.pallas_call(
        functools.partial(_mem_attn_kernel, heads=heads),
        out_shape=jax.ShapeDtypeStruct((nseq * rows_per_seq, W), BF16),
        grid=(nseq, nt),
        in_specs=[pl.BlockSpec((tm, W), lambda b, i: (rb0 + b * nt + i, col_block)), mem_spec, mem_spec],
        out_specs=pl.BlockSpec((tm, W), lambda b, i: (b * nt + i, 0)),
        compiler_params=_params(("parallel", "parallel")),
        name="mem_attend",
    )(proj, mk, mv)


def _bdot(a, b):
    return _dot(a.astype(BF16), b.astype(BF16))


def _unit_lower_solve(As, rhss, n, L):
    row = lax.broadcasted_iota(jnp.int32, (n, n), 0)
    col = lax.broadcasted_iota(jnp.int32, (n, n), 1)
    shift = SOLVE_BLOCK.bit_length() - 1
    same = (row >> shift) == (col >> shift)
    es = [jnp.where(same, -A, 0.0) for A in As]
    offs = [jnp.where(same, 0.0, A) for A in As]
    ps = es
    span = 2
    while span < SOLVE_BLOCK:
        ps = [_bdot(p, p) for p in ps]
        es = [e + p + _bdot(e, p) for e, p in zip(es, ps)]
        span *= 2
    ys = [r + _bdot(e, r) for e, r in zip(es, rhss)]
    nblocks = L // SOLVE_BLOCK
    if nblocks > 1:
        powers = [[o + _bdot(e, o) for e, o in zip(es, offs)]]
        span = 2
        while span < nblocks:
            powers.append([_bdot(p, p) for p in powers[-1]])
            span *= 2
        for pw in reversed(powers[1:]):
            ys = [y + _bdot(p, y) for p, y in zip(pw, ys)]
        ys = [y - _bdot(p, y) for p, y in zip(powers[0], ys)]
    return ys


def _gdn_kernel(q_ref, k_ref, v_ref, z_ref, gb_ref, grow_ref, hist_ref, cw_ref, s0_ref, gn_ref,
                o_ref, sfin_ref, conv_ref, s_scr, prev_scr, *, L, Hs, H):
    c = pl.program_id(1)
    n = Hs * L
    tokd = H * HEAD_DIM
    lshift = L.bit_length() - 1

    @pl.when(c == 0)
    def _():
        s_scr[...] = s0_ref[0]
        prev_scr[...] = jnp.zeros_like(prev_scr)
        prev_scr[L - SUBLANES:L, :] = hist_ref[0]

    rowi = lax.broadcasted_iota(jnp.int32, (L, 1), 0)

    def conv_silu(t, x_ref):
        cs = slice(t * tokd, (t + 1) * tokd)
        x = x_ref[...]
        prev = prev_scr[:, cs]
        w = cw_ref[:, cs]
        y = None
        for s in range(CONV_W - 1, 0, -1):
            sh = jnp.where(rowi >= s, pltpu.roll(x, s, 0), pltpu.roll(prev, s, 0))
            term = sh * w[CONV_W - 1 - s:CONV_W - s]
            y = term if y is None else y + term
        y = y + x * w[CONV_W - 1:CONV_W]
        prev_scr[:, cs] = x
        return y * _sigmoid(y)

    yq = conv_silu(0, q_ref)
    yk = conv_silu(1, k_ref)
    yv = conv_silu(2, v_ref)

    @pl.when(c == pl.num_programs(1) - 1)
    def _():
        conv_ref[0] = prev_scr[L - SUBLANES:L, :]

    r2 = lax.broadcasted_iota(jnp.int32, (n, n), 0)
    c2 = lax.broadcasted_iota(jnp.int32, (n, n), 1)
    same = (r2 >> lshift) == (c2 >> lshift)
    incl = same & (r2 >= c2)
    strict = same & (r2 > c2)
    tril = jnp.where(incl, 1.0, 0.0).astype(BF16)
    triu = jnp.where(same & (r2 <= c2), 1.0, 0.0).astype(BF16)
    lane = lax.broadcasted_iota(jnp.int32, (n, LANES), 1)
    head_in_group = lax.broadcasted_iota(jnp.int32, (n, LANES), 0) >> lshift
    gb4 = jnp.concatenate([gb_ref[...]] * Hs, axis=0)
    gnorm = gn_ref[...]

    def stack(t, heads):
        return jnp.concatenate([t[:, h * HEAD_DIM:(h + 1) * HEAD_DIM] for h in heads], axis=0)

    groups = [[gi * Hs + j for j in range(Hs)] for gi in range(H // Hs)]
    headrows = [head_in_group + hs[0] for hs in groups]
    rsl = [slice(j * L, (j + 1) * L) for j in range(Hs)]

    def dot_split3(parts, m, left):
        return sum(_dot(m, p) if left else _dot(p, m) for p in parts)

    Gs = [jnp.sum(dot_split3(_split3(jnp.where(lane == hr, gb4, 0.0)), tril, True), axis=1, keepdims=True)
          for hr in headrows]
    Grs = [dot_split3(_split3(grow_ref[0, gi]), triu, False)[0:1] for gi in range(len(groups))]
    betas = [jnp.sum(jnp.where(lane == hr + H, gb4, 0.0), axis=1, keepdims=True) for hr in headrows]
    glasts = [[G[(j + 1) * L - 1:(j + 1) * L] for j in range(Hs)] for G in Gs]
    GLs = [jnp.concatenate([jnp.broadcast_to(g, (L, 1)) for g in gl], axis=0) for gl in glasts]

    def l2n(t):
        return t * lax.rsqrt(jnp.sum(t * t, axis=-1, keepdims=True) + EPS)

    qs = [l2n(stack(yq, hs)) * SCALE for hs in groups]
    ks = [l2n(stack(yk, hs)) for hs in groups]
    vs = [stack(yv, hs) for hs in groups]
    eGs = [jnp.exp(G) for G in Gs]
    decays = [jnp.exp(jnp.where(incl, G - Gr, -jnp.inf)) for G, Gr in zip(Gs, Grs)]
    kbs = [k.astype(BF16) for k in ks]
    kks = [_dot_nt(kb, kb) for kb in kbs]
    qks = [_dot_nt(q.astype(BF16), kb) for q, kb in zip(qs, kbs)]
    As = [jnp.where(strict, b * kk * d, 0.0) for b, kk, d in zip(betas, kks, decays)]
    rhss = [jnp.concatenate([b * v, (b * eG) * k], axis=-1) for b, v, eG, k in zip(betas, vs, eGs, ks)]
    sols = _unit_lower_solve(As, rhss, n, L)
    us = [s[:, :HEAD_DIM] for s in sols]
    ws = [s[:, HEAD_DIM:].astype(BF16) for s in sols]
    qes = [(q * eG).astype(BF16) for q, eG in zip(qs, eGs)]
    kds = [(k * jnp.exp(GL - G)).astype(BF16) for k, GL, G in zip(ks, GLs, Gs)]
    attns = [(qk * d).astype(BF16) for qk, d in zip(qks, decays)]

    sbs = [[s_scr[h].astype(BF16) for h in hs] for hs in groups]
    v_news = [jnp.concatenate([u[rs] - _dot(w[rs], sb) for rs, sb in zip(rsl, sbg)], axis=0).astype(BF16)
              for u, w, sbg in zip(us, ws, sbs)]
    outs = [jnp.concatenate([_dot(qe[rs], sb) for rs, sb in zip(rsl, sbg)], axis=0) + _dot(at, vn)
            for qe, sbg, at, vn in zip(qes, sbs, attns, v_news)]
    for hs, gl, kd, vn in zip(groups, glasts, kds, v_news):
        for j, h in enumerate(hs):
            s_scr[h] = jnp.exp(gl[j]) * s_scr[h] + _dot_tn(kd[rsl[j]], vn[rsl[j]])

    z = z_ref[...]
    for hs, out in zip(groups, outs):
        zz = stack(z, hs)
        o = (_rms(out) * gnorm * (zz * _sigmoid(zz))).astype(o_ref.dtype)
        for j, h in enumerate(hs):
            o_ref[:, h * HEAD_DIM:(h + 1) * HEAD_DIM] = o[rsl[j]]

    @pl.when(c == pl.num_programs(1) - 1)
    def _():
        sfin_ref[0] = s_scr[...]


def gdn_mix(proj, gb, hist8, cw8, s0, gnorm, *, row0, nseq, T, L, H):
    Hs = GDN_STACK
    NC = T // L
    NG = H // Hs
    n = Hs * L
    tokd = H * HEAD_DIM
    rb0 = row0 // L
    rows = nseq * T
    g_rows = gb[row0:row0 + rows, :H].reshape(nseq * NC, L, NG, Hs).transpose(0, 2, 3, 1)
    g_rows = jnp.pad(g_rows.reshape(nseq * NC, NG, 1, n), ((0, 0), (0, 0), (0, SUBLANES - 1), (0, 0)))

    def col(group):
        return lambda b, c: (rb0 + b * NC + c, group)

    return pl.pallas_call(
        functools.partial(_gdn_kernel, L=L, Hs=Hs, H=H),
        out_shape=(jax.ShapeDtypeStruct((rows, tokd), BF16),
                   jax.ShapeDtypeStruct((nseq, H, HEAD_DIM, HEAD_DIM), F32),
                   jax.ShapeDtypeStruct((nseq, SUBLANES, 3 * tokd), F32)),
        grid=(nseq, NC),
        in_specs=[pl.BlockSpec((L, tokd), col(0)), pl.BlockSpec((L, tokd), col(1)),
                  pl.BlockSpec((L, tokd), col(2)), pl.BlockSpec((L, tokd), col(3)),
                  pl.BlockSpec((L, LANES), lambda b, c: (rb0 + b * NC + c, 0)),
                  pl.BlockSpec((1, NG, SUBLANES, n), lambda b, c: (b * NC + c, 0, 0, 0)),
                  pl.BlockSpec((1, SUBLANES, 3 * tokd), lambda b, c: (b, 0, 0)),
                  pl.BlockSpec((SUBLANES, 3 * tokd), lambda b, c: (0, 0)),
                  pl.BlockSpec((1, H, HEAD_DIM, HEAD_DIM), lambda b, c: (b, 0, 0, 0)),
                  pl.BlockSpec((1, HEAD_DIM), lambda b, c: (0, 0))],
        out_specs=(pl.BlockSpec((L, tokd), lambda b, c: (b * NC + c, 0)),
                   pl.BlockSpec((1, H, HEAD_DIM, HEAD_DIM), lambda b, c: (b, 0, 0, 0)),
                   pl.BlockSpec((1, SUBLANES, 3 * tokd), lambda b, c: (b, 0, 0))),
        scratch_shapes=[pltpu.VMEM((H, HEAD_DIM, HEAD_DIM), F32),
                        pltpu.VMEM((L, 3 * tokd), F32)],
        compiler_params=_params(("parallel", "arbitrary")),
        name="gdn_mix",
    )(proj, proj, proj, proj, gb, g_rows, hist8, cw8, s0, gnorm.reshape(1, HEAD_DIM))


def _cumsum_kernel(x_ref, o_ref, carry_ref, *, tb):
    @pl.when(pl.program_id(0) == 0)
    def _():
        carry_ref[...] = jnp.zeros_like(carry_ref)

    r = lax.broadcasted_iota(jnp.int32, (tb, tb), 0)
    c = lax.broadcasted_iota(jnp.int32, (tb, tb), 1)
    triu = jnp.where(r <= c, 1.0, 0.0).astype(BF16)
    parts = _split3(x_ref[...])
    out = (_dot(parts[0], triu) + _dot(parts[1], triu) + _dot(parts[2], triu)) + carry_ref[...]
    o_ref[...] = out
    carry_ref[...] = out[:, tb - 1:tb]


def cumsum_lanes(x, *, tb):
    R, T = x.shape
    return pl.pallas_call(
        functools.partial(_cumsum_kernel, tb=tb),
        out_shape=jax.ShapeDtypeStruct((R, T), F32),
        grid=(T // tb,),
        in_specs=[pl.BlockSpec((R, tb), lambda i: (0, i))],
        out_specs=pl.BlockSpec((R, tb), lambda i: (0, i)),
        scratch_shapes=[pltpu.VMEM((R, 1), F32)],
        compiler_params=_params(("arbitrary",)),
        name="cumsum_lanes",
    )(x)


def _fox_prompt_kernel(q_ref, gate_ref, k_ref, v_ref, cq_ref, ck_ref, o_ref,
                       qb_scr, cq_scr, m_scr, acc_scr, *, t):
    h = pl.program_id(1)
    qi = pl.program_id(2)
    ki = pl.program_id(3)

    @pl.when(ki == 0)
    def _():
        qb_scr[...] = q_ref[...].astype(BF16)
        lane = lax.broadcasted_iota(jnp.int32, cq_ref.shape, 1)
        cq_scr[...] = jnp.sum(jnp.where(lane == h, cq_ref[...], 0.0), axis=1, keepdims=True) * LOG2E
        m_scr[...] = jnp.full_like(m_scr, -jnp.inf)
        acc_scr[...] = jnp.zeros_like(acc_scr)

    def step(masked):
        s = _dot_nt(qb_scr[...], k_ref[...]) * (SCALE * LOG2E) + (cq_scr[...] - ck_ref[0] * LOG2E)
        if masked:
            row = lax.broadcasted_iota(jnp.int32, (t, t), 0)
            col = lax.broadcasted_iota(jnp.int32, (t, t), 1)
            s = jnp.where(col <= row, s, -jnp.inf)
        m_old = m_scr[...]
        m_new = jnp.maximum(m_old, jnp.max(s, axis=-1, keepdims=True))
        p = jnp.exp2(s - m_new).astype(BF16)
        v1 = jnp.concatenate([v_ref[...], jnp.ones((t, LANES), BF16)], axis=1)
        acc_scr[...] = jnp.exp2(m_old - m_new) * acc_scr[...] + _dot(p, v1)
        m_scr[...] = m_new

    @pl.when(ki < qi)
    def _():
        step(False)

    @pl.when(ki == qi)
    def _():
        step(True)
        acc = acc_scr[...]
        o = acc[:, :HEAD_DIM] / acc[:, HEAD_DIM:HEAD_DIM + 1]
        o_ref[...] = (o * _sigmoid(gate_ref[...].astype(F32))).astype(o_ref.dtype)


def fox_prompt(proj, karr, varr, c_col, c_rows, *, nseq, T, H, t):
    nb = T // t
    return pl.pallas_call(
        functools.partial(_fox_prompt_kernel, t=t),
        out_shape=jax.ShapeDtypeStruct((nseq * T, H * HEAD_DIM), BF16),
        grid=(nseq, H, nb, nb),
        in_specs=[pl.BlockSpec((t, HEAD_DIM), lambda b, h, qi, ki: (b * nb + qi, h)),
                  pl.BlockSpec((t, HEAD_DIM), lambda b, h, qi, ki: (b * nb + qi, H + h)),
                  pl.BlockSpec((t, HEAD_DIM), lambda b, h, qi, ki: (b * nb + jnp.minimum(ki, qi), h)),
                  pl.BlockSpec((t, HEAD_DIM), lambda b, h, qi, ki: (b * nb + jnp.minimum(ki, qi), h)),
                  pl.BlockSpec((t, H), lambda b, h, qi, ki: (b * nb + qi, 0)),
                  pl.BlockSpec((1, 1, t), lambda b, h, qi, ki: (b * H + h, 0, jnp.minimum(ki, qi)))],
        out_specs=pl.BlockSpec((t, HEAD_DIM), lambda b, h, qi, ki: (b * nb + qi, h)),
        scratch_shapes=[pltpu.VMEM((t, HEAD_DIM), BF16), pltpu.VMEM((t, 1), F32),
                        pltpu.VMEM((t, 1), F32), pltpu.VMEM((t, 2 * HEAD_DIM), F32)],
        compiler_params=_params(("parallel", "parallel", "parallel", "arbitrary")),
        name="fox_prompt",
    )(proj, proj, karr, varr, c_col, c_rows)


def _fox_sample_kernel(q_ref, gate_ref, ck_ref, cv_ref, kn_ref, vn_ref, cq_ref, cc_ref, cn_ref, o_ref,
                       m_scr, l_scr, acc_scr, *, H, T):
    ki = pl.program_id(1)

    @pl.when(ki == 0)
    def _():
        m_scr[...] = jnp.full_like(m_scr, -jnp.inf)
        l_scr[...] = jnp.zeros_like(l_scr)
        acc_scr[...] = jnp.zeros_like(acc_scr)

    cq = cq_ref[...]

    sls = [slice(h * HEAD_DIM, (h + 1) * HEAD_DIM) for h in range(H)]

    def update(k_of, v_of, ck_of, mask):
        ss = [_dot_nt(q_ref[:, sl], k_of(sl)) * SCALE + (cq[:, h:h + 1] - ck_of(h)) for h, sl in enumerate(sls)]
        if mask is not None:
            ss = [jnp.where(mask, s, -jnp.inf) for s in ss]
        m_olds = [m_scr[h] for h in range(H)]
        m_news = [jnp.maximum(mo, jnp.max(s, axis=-1, keepdims=True)) for mo, s in zip(m_olds, ss)]
        alphas = [jnp.exp(mo - mn) for mo, mn in zip(m_olds, m_news)]
        ps = [jnp.exp(s - mn) for s, mn in zip(ss, m_news)]
        pvs = [_dot(p.astype(BF16), v_of(sl)) for p, sl in zip(ps, sls)]
        for h, sl in enumerate(sls):
            l_scr[h] = alphas[h] * l_scr[h] + jnp.sum(ps[h], axis=-1, keepdims=True)
            acc_scr[:, sl] = alphas[h] * acc_scr[:, sl] + pvs[h]
            m_scr[h] = m_news[h]

    update(lambda sl: ck_ref[0, :, sl], lambda sl: cv_ref[0, :, sl], lambda h: cc_ref[0, h:h + 1, :], None)

    @pl.when(ki == pl.num_programs(1) - 1)
    def _():
        causal = (lax.broadcasted_iota(jnp.int32, (T, T), 1) <= lax.broadcasted_iota(jnp.int32, (T, T), 0))
        update(lambda sl: kn_ref[:, sl], lambda sl: vn_ref[:, sl], lambda h: cn_ref[0, h:h + 1, :], causal)
        for h, sl in enumerate(sls):
            o = acc_scr[:, sl] / l_scr[h] * _sigmoid(gate_ref[:, sl].astype(F32))
            o_ref[:, sl] = o.astype(o_ref.dtype)


def fox_sample(proj, cache_k, cache_v, k_new, v_new, c_col, c_cache, c_new, *, row0, nseq, T, H, tk):
    P = cache_k.shape[1]
    W = H * HEAD_DIM
    rb0 = row0 // T
    cache_spec = pl.BlockSpec((1, tk, W), lambda b, i: (b, i, 0))
    new_spec = pl.BlockSpec((T, W), lambda b, i: (b, 0))
    return pl.pallas_call(
        functools.partial(_fox_sample_kernel, H=H, T=T),
        out_shape=jax.ShapeDtypeStruct((nseq * T, W), BF16),
        grid=(nseq, P // tk),
        in_specs=[pl.BlockSpec((T, W), lambda b, i: (rb0 + b, 0)),
                  pl.BlockSpec((T, W), lambda b, i: (rb0 + b, 1)),
                  cache_spec, cache_spec, new_spec, new_spec,
                  pl.BlockSpec((T, H), lambda b, i: (b, 0)),
                  pl.BlockSpec((1, H, tk), lambda b, i: (b, 0, i)),
                  pl.BlockSpec((1, H, T), lambda b, i: (b, 0, 0))],
        out_specs=pl.BlockSpec((T, W), lambda b, i: (b, 0)),
        scratch_shapes=[pltpu.VMEM((H, T, 1), F32), pltpu.VMEM((H, T, 1), F32), pltpu.VMEM((T, W), F32)],
        compiler_params=_params(("parallel", "arbitrary")),
        name="fox_sample",
    )(proj, proj, cache_k, cache_v, k_new, v_new, c_col, c_cache, c_new)


def _pad_cols(a, width):
    return jnp.pad(a, ((0, 0), (0, width - a.shape[1])))


def kernel(x_prompt, x_sample, state_gdn, state_conv, cache_k, cache_v, cache_logf, cache_mem_k, cache_mem_v, mem_prompt, norm_mix_pre, norm_mix_post, norm_mlp_pre, norm_mlp_post, w_in_a, conv_w_a, a_log, dt_bias, gdn_norm, w_in_b, norm_kv, w_kvf, b_f, norm_mem, w_mem_kv, w_o, w_up, w_down):
    Bp, Tp, D = x_prompt.shape
    Bs, Ts, _ = x_sample.shape
    n_a = w_in_a.shape[0]
    depth = w_o.shape[0]
    H = a_log.shape[1]
    tokd = H * HEAD_DIM
    qkvd = 3 * tokd
    P = cache_k.shape[1]
    nmem = mem_prompt.shape[1]
    mh = cache_mem_k.shape[3]
    memd = mh * HEAD_DIM
    Mp, Ms = Bp * Tp, Bs * Ts
    M = Mp + Ms
    TM = next(t for t in (1024, 512, 256, 128, 64, 32) if M % t == 0)
    TMH = max(TM // 2, 32)
    HR = -(-H // SUBLANES) * SUBLANES

    x = jnp.concatenate([x_prompt.reshape(Mp, D), x_sample.reshape(Ms, D)], axis=0)

    mem_rows = mem_prompt.reshape(Bp * nmem, D)
    pmk, pmv = [], []
    for l in range(depth):
        wkv = w_mem_kv[l].astype(BF16)
        pmk.append(norm_matmul(mem_rows, norm_mem[l], wkv[:, :memd], tm=min(TM, Bp * nmem), tn=memd))
        pmv.append(norm_matmul(mem_rows, norm_mem[l], wkv[:, memd:], tm=min(TM, Bp * nmem), tn=memd))
    p_mem_k = jnp.stack(pmk).reshape(depth, Bp, nmem, mh, HEAD_DIM)
    p_mem_v = jnp.stack(pmv).reshape(depth, Bp, nmem, mh, HEAD_DIM)

    def hist8(h):
        return jnp.pad(h, ((0, 0), (SUBLANES - (CONV_W - 1), 0), (0, 0)))

    cache_k2 = cache_k.astype(BF16).reshape(Bs, P, tokd)
    cache_v2 = cache_v.astype(BF16).reshape(Bs, P, tokd)

    p_gdn, s_gdn, p_conv, s_conv = [], [], [], []
    lane = jnp.arange(LANES)
    for l in range(depth):
        if l == n_a:
            wk = w_kvf[:, :tokd].astype(BF16)
            wv = w_kvf[:, tokd:2 * tokd].astype(BF16)
            wf = _pad_cols(w_kvf[:, 2 * tokd:], LANES).astype(BF16)
            kp2, p_k = kv_proj(x, norm_kv, wk, row0=0, rows=Mp, tm=TMH, H=H)
            vp2, p_v = kv_proj(x, norm_kv, wv, row0=0, rows=Mp, tm=TMH, H=H)
            ks2, s_k = kv_proj(x, norm_kv, wk, row0=Mp, rows=Ms, tm=TMH, H=H)
            vs2, s_v = kv_proj(x, norm_kv, wv, row0=Mp, rows=Ms, tm=TMH, H=H)
            pf = jnp.zeros((SUBLANES, LANES), F32).at[0, :H].set(b_f)
            logf = norm_matmul(x, norm_kv, wf, tm=TM, tn=LANES, epilogue=_ep_log_forget, p=pf)[:, :H]
            lf_p = logf[:Mp].reshape(Bp, Tp, H)
            lf_s = logf[Mp:].reshape(Bs, Ts, H)

            def rows_of(a):
                a = jnp.pad(a.transpose(0, 2, 1), ((0, 0), (0, HR - H), (0, 0)))
                return a.reshape(a.shape[0] * HR, a.shape[2])

            cp_rows = cumsum_lanes(rows_of(lf_p), tb=min(Tp, 512)).reshape(Bp, HR, Tp)[:, :H]
            cp_col = cp_rows.transpose(0, 2, 1).reshape(Mp, H)
            tot = P + Ts
            tot_pad = -(-tot // LANES) * LANES
            lf_all = jnp.pad(jnp.concatenate([cache_logf, lf_s], axis=1), ((0, 0), (0, tot_pad - tot), (0, 0)))
            cs_rows = cumsum_lanes(rows_of(lf_all), tb=LANES).reshape(Bs, HR, tot_pad)[:, :H]
            cs_cache = cs_rows[:, :, :P]
            cs_new = cs_rows[:, :, P:tot]
            cs_col = cs_new.transpose(0, 2, 1).reshape(Ms, H)

        wo = w_o[l].astype(BF16)
        if l < n_a:
            w = w_in_a[l]
            o1 = qkvd + tokd
            w_main = jnp.concatenate([w[:, :o1], w[:, o1 + 2 * H:]], axis=1).astype(BF16)
            w_ab = _pad_cols(w[:, o1:o1 + 2 * H], LANES).astype(BF16)
            proj = norm_matmul(x, norm_mix_pre[l], w_main, tm=TM, tn=512)
            pg = jnp.zeros((SUBLANES, LANES), F32)
            pg = pg.at[0, :H].set(a_log[l]).at[1, :H].set(dt_bias[l]).at[2].set((lane < H).astype(F32))
            gb = norm_matmul(x, norm_mix_pre[l], w_ab, tm=TM, tn=LANES, epilogue=_ep_gdn_gates, p=pg)
            cw8 = jnp.pad(conv_w_a[l], ((0, SUBLANES - CONV_W), (0, 0)))
            tok_p, sp, cp = gdn_mix(proj, gb, hist8(jnp.zeros((Bp, CONV_W - 1, qkvd), F32)), cw8,
                                    jnp.zeros((Bp, H, HEAD_DIM, HEAD_DIM), F32), gdn_norm[l],
                                    row0=0, nseq=Bp, T=Tp, L=min(Tp, GDN_CHUNK), H=H)
            tok_s, ss, cs = gdn_mix(proj, gb, hist8(state_conv[l]), cw8, state_gdn[l], gdn_norm[l],
                                    row0=Mp, nseq=Bs, T=Ts, L=min(Ts, GDN_CHUNK), H=H)
            p_gdn.append(sp)
            s_gdn.append(ss)
            p_conv.append(cp[:, SUBLANES - (CONV_W - 1):])
            s_conv.append(cs[:, SUBLANES - (CONV_W - 1):])
            mq_block = (qkvd + tokd) // memd
        else:
            proj = norm_matmul(x, norm_mix_pre[l], w_in_b[l - n_a].astype(BF16), tm=TM, tn=512, out_dtype=BF16)
            tok_p = fox_prompt(proj, kp2, vp2, cp_col, cp_rows.reshape(Bp * H, 1, Tp),
                               nseq=Bp, T=Tp, H=H, t=min(Tp, 1024))
            tok_s = fox_sample(proj, cache_k2, cache_v2, ks2, vs2, cs_col, cs_cache, cs_new,
                               row0=Mp, nseq=Bs, T=Ts, H=H, tk=min(P, 1024))
            mq_block = 2 * tokd // memd
        tok = jnp.concatenate([tok_p, tok_s], axis=0)
        mem_p = mem_attend(proj, p_mem_k, p_mem_v, l, row0=0, rows_per_seq=Tp, tm=min(Tp, 512),
                           col_block=mq_block)
        mem_s = mem_attend(proj, cache_mem_k, cache_mem_v, l, row0=Mp, rows_per_seq=Ts, tm=Ts,
                           col_block=mq_block)
        mem_o = jnp.concatenate([mem_p, mem_s], axis=0)
        x = out_proj(tok, mem_o, x, wo[:tokd], wo[tokd:], norm_mix_post[l], tm=TMH)
        x = mlp(x, norm_mlp_pre[l], norm_mlp_post[l], w_up[l].astype(BF16), w_down[l].astype(BF16),
                tm=TMH, tf=512)

    y_prompt = x[:Mp].reshape(Bp, Tp, D)
    y_sample = x[Mp:].reshape(Bs, Ts, D)
    return (y_prompt, y_sample, jnp.stack(p_gdn), jnp.stack(p_conv),
            p_k.reshape(Bp, Tp, H, HEAD_DIM), p_v.reshape(Bp, Tp, H, HEAD_DIM), lf_p, p_mem_k, p_mem_v,
            jnp.stack(s_gdn), jnp.stack(s_conv),
            s_k.reshape(Bs, Ts, H, HEAD_DIM), s_v.reshape(Bs, Ts, H, HEAD_DIM), lf_s)
```

```python
import functools
import math

import jax
import jax.numpy as jnp
from jax import lax
from jax.experimental import pallas as pl
from jax.experimental.pallas import tpu as pltpu

F32 = jnp.float32
BF16 = jnp.bfloat16
EPS = 1e-6
HEAD_DIM = 128
LANES = 128
SUBLANES = 8
SCALE = HEAD_DIM ** -0.5
LOG2E = math.log2(math.e)
CONV_W = 4
GDN_CHUNK = 64
GDN_STACK = 4
SOLVE_BLOCK = 16
HI = lax.Precision.HIGHEST
VMEM_LIMIT = 56 * 1024 * 1024


def _params(sem, vmem=VMEM_LIMIT):
    return pltpu.CompilerParams(dimension_semantics=sem, vmem_limit_bytes=vmem)


def _rms(x):
    return x * lax.rsqrt(jnp.mean(x * x, axis=-1, keepdims=True) + EPS)


def _sigmoid(x):
    return 1.0 / (1.0 + jnp.exp(-x))


def _softplus(x):
    return jnp.maximum(x, 0.0) + jnp.log1p(jnp.exp(-jnp.abs(x)))


def _dot(a, b, precision=None):
    return jnp.dot(a, b, preferred_element_type=F32, precision=precision)


def _dot_tn(a, b):
    return lax.dot_general(a, b, (((0,), (0,)), ((), ())), preferred_element_type=F32)


def _dot_nt(a, b):
    return lax.dot_general(a, b, (((1,), (1,)), ((), ())), preferred_element_type=F32)


def _split2(x):
    hi = x.astype(BF16)
    return hi, (x - hi.astype(F32)).astype(BF16)


def _split3(x):
    hi = x.astype(BF16)
    r = x - hi.astype(F32)
    mid = r.astype(BF16)
    return hi, mid, (r - mid.astype(F32)).astype(BF16)


def _dot3(a, b):
    return _dot(a[0], b[0]) + (_dot(a[0], b[1]) + _dot(a[1], b[0]))


def _ep_none(acc, p):
    return acc


def _ep_gdn_gates(acc, p):
    g = -jnp.exp(p[0:1]) * _softplus(acc + p[1:2])
    return jnp.where(p[2:3] > 0.5, g, _sigmoid(acc))


def _ep_log_forget(acc, p):
    return -_softplus(-(acc + p[0:1]))


def _norm_matmul_kernel(x_ref, g_ref, w_ref, p_ref, o_ref, h_ref, *, epilogue):
    @pl.when(pl.program_id(1) == 0)
    def _():
        h_ref[...] = (_rms(x_ref[...]) * g_ref[...]).astype(BF16)

    acc = _dot(h_ref[...], w_ref[...])
    o_ref[...] = epilogue(acc, p_ref[...]).astype(o_ref.dtype)


def _col_tile(n, cap=1792):
    return max(c for c in range(LANES, min(n, cap) + 1, LANES) if n % c == 0)


def norm_matmul(x, g, w, *, tm, tn, layer=0, out_dtype=F32, epilogue=_ep_none, p=None):
    M, K = x.shape
    N = w.shape[2]
    if p is None:
        p = jnp.zeros((SUBLANES, tn), F32)
    return pl.pallas_call(
        functools.partial(_norm_matmul_kernel, epilogue=epilogue),
        out_shape=jax.ShapeDtypeStruct((M, N), out_dtype),
        grid=(M // tm, N // tn),
        in_specs=[pl.BlockSpec((tm, K), lambda i, j: (i, 0)),
                  pl.BlockSpec((1, K), lambda i, j: (0, 0)),
                  pl.BlockSpec((None, K, tn), lambda i, j: (layer, 0, j)),
                  pl.BlockSpec((SUBLANES, tn), lambda i, j: (0, 0))],
        out_specs=pl.BlockSpec((tm, tn), lambda i, j: (i, j)),
        scratch_shapes=[pltpu.VMEM((tm, K), BF16)],
        compiler_params=_params(("parallel", "arbitrary")),
        name="norm_matmul",
    )(x, g.reshape(1, K), w, p)


def _kv_proj_kernel(x_ref, g_ref, w_ref, o2_ref, o4_ref, *, H, ns, tt):
    h = (_rms(x_ref[...]) * g_ref[...]).astype(BF16)
    acc = _dot(h, w_ref[...])
    o2_ref[...] = acc.astype(o2_ref.dtype)
    for s in range(ns):
        for hh in range(H):
            o4_ref[s, hh] = acc[s * tt:(s + 1) * tt, hh * HEAD_DIM:(hh + 1) * HEAD_DIM]


def kv_proj(x, g, w, *, row0, nseq, T, tm, H):
    K = x.shape[1]
    N = w.shape[1]
    rows = nseq * T
    rb0 = row0 // tm
    ns, tt = (tm // T, T) if tm >= T else (1, tm)
    parts = T // tt
    return pl.pallas_call(
        functools.partial(_kv_proj_kernel, H=H, ns=ns, tt=tt),
        out_shape=(jax.ShapeDtypeStruct((rows, N), BF16),
                   jax.ShapeDtypeStruct((nseq, H, T, HEAD_DIM), F32)),
        grid=(rows // tm,),
        in_specs=[pl.BlockSpec((tm, K), lambda i: (rb0 + i, 0)),
                  pl.BlockSpec((1, K), lambda i: (0, 0)),
                  pl.BlockSpec((K, N), lambda i: (0, 0))],
        out_specs=(pl.BlockSpec((tm, N), lambda i: (i, 0)),
                   pl.BlockSpec((ns, H, tt, HEAD_DIM), lambda i: (i // parts, 0, i % parts, 0))),
        compiler_params=_params(("parallel",)),
        name="kv_proj",
    )(x, g.reshape(1, K), w)


def _out_proj_kernel(tok_ref, mem_ref, x_ref, wa_ref, wb_ref, g_ref, o_ref):
    mix = _dot(tok_ref[...], wa_ref[...]) + _dot(mem_ref[...], wb_ref[...])
    o_ref[...] = x_ref[...] + _rms(mix) * g_ref[...]


def out_proj(tok, mem, x, w, layer, g, *, tm):
    M, D = x.shape
    Ka, Kb = tok.shape[1], mem.shape[1]
    assert Ka % Kb == 0
    return pl.pallas_call(
        _out_proj_kernel,
        out_shape=jax.ShapeDtypeStruct((M, D), F32),
        grid=(M // tm,),
        in_specs=[pl.BlockSpec((tm, Ka), lambda i: (i, 0)),
                  pl.BlockSpec((tm, Kb), lambda i: (i, 0)),
                  pl.BlockSpec((tm, D), lambda i: (i, 0)),
                  pl.BlockSpec((None, Ka, D), lambda i: (layer, 0, 0)),
                  pl.BlockSpec((None, Kb, D), lambda i: (layer, Ka // Kb, 0)),
                  pl.BlockSpec((1, D), lambda i: (0, 0))],
        out_specs=pl.BlockSpec((tm, D), lambda i: (i, 0)),
        compiler_params=_params(("parallel",)),
        name="out_proj",
    )(tok, mem, x, w, w, g.reshape(1, D))


def _mlp_kernel(x_ref, gpre_ref, gpost_ref, wup_ref, wdn_ref, o_ref, h_ref, acc_ref):
    j = pl.program_id(1)

    @pl.when(j == 0)
    def _():
        h_ref[...] = (_rms(x_ref[...]) * gpre_ref[...]).astype(BF16)
        acc_ref[...] = jnp.zeros_like(acc_ref)

    up = _dot(h_ref[...], wup_ref[...])
    act = jnp.square(jnp.maximum(up, 0.0)).astype(BF16)
    acc_ref[...] += _dot(act, wdn_ref[...])

    @pl.when(j == pl.num_programs(1) - 1)
    def _():
        o_ref[...] = x_ref[...] + _rms(acc_ref[...]) * gpost_ref[...]


def mlp(x, gpre, gpost, wup, wdn, layer, *, tm, tf):
    M, D = x.shape
    FF = wup.shape[2]
    return pl.pallas_call(
        _mlp_kernel,
        out_shape=jax.ShapeDtypeStruct((M, D), F32),
        grid=(M // tm, FF // tf),
        in_specs=[pl.BlockSpec((tm, D), lambda i, j: (i, 0)),
                  pl.BlockSpec((1, D), lambda i, j: (0, 0)),
                  pl.BlockSpec((1, D), lambda i, j: (0, 0)),
                  pl.BlockSpec((None, D, tf), lambda i, j: (layer, 0, j)),
                  pl.BlockSpec((None, tf, D), lambda i, j: (layer, j, 0))],
        out_specs=pl.BlockSpec((tm, D), lambda i, j: (i, 0)),
        scratch_shapes=[pltpu.VMEM((tm, D), BF16), pltpu.VMEM((tm, D), F32)],
        compiler_params=_params(("parallel", "arbitrary")),
        name="mlp",
    )(x, gpre.reshape(1, D), gpost.reshape(1, D), wup, wdn)


def _row_range_output(base, total_rows, width, n_inputs):
    spec = pl.BlockSpec(memory_space=pl.ANY)
    if base is None:
        return jnp.zeros((SUBLANES, LANES), BF16), spec, {}
    assert base.shape == (total_rows, width) and base.dtype == BF16
    return base, spec, {n_inputs: 0}


def _mem_attn_kernel(q_ref, mk_ref, mv_ref, base_ref, o_ref, *, heads):
    del base_ref
    for h in range(heads):
        sl = slice(h * HEAD_DIM, (h + 1) * HEAD_DIM)
        q = q_ref[:, sl].astype(BF16)
        k = mk_ref[0, 0, :, h, :].astype(BF16)
        v = mv_ref[0, 0, :, h, :].astype(BF16)
        s = _dot_nt(q, k) * SCALE
        e = jnp.exp(s - jnp.max(s, axis=-1, keepdims=True))
        p = e / jnp.sum(e, axis=-1, keepdims=True)
        o_ref[:, sl] = _dot(p.astype(BF16), v).astype(o_ref.dtype)


def mem_attend(proj, mk, mv, layer, base, *, row0, rows_per_seq, tm, col_block):
    _, nseq, nmem, heads, _ = mk.shape
    W = heads * HEAD_DIM
    nt = rows_per_seq // tm
    rb0 = row0 // tm
    mem_spec = pl.BlockSpec((1, 1, nmem, heads, HEAD_DIM), lambda b, i: (layer, b, 0, 0, 0))
    base, base_spec, aliases = _row_range_output(base, proj.shape[0], W, 3)
    return pl.pallas_call(
        functools.partial(_mem_attn_kernel, heads=heads),
        out_shape=jax.ShapeDtypeStruct((proj.shape[0], W), BF16),
        grid=(nseq, nt),
        in_specs=[pl.BlockSpec((tm, W), lambda b, i: (rb0 + b * nt + i, col_block)), mem_spec, mem_spec,
                  base_spec],
        out_specs=pl.BlockSpec((tm, W), lambda b, i: (rb0 + b * nt + i, 0)),
        input_output_aliases=aliases,
        compiler_params=_params(("parallel", "parallel")),
        name="mem_attend",
    )(proj, mk, mv, base)


def _bdot(a, b):
    return _dot(a.astype(BF16), b.astype(BF16))


def _unit_lower_solve(As, rhss, n, L):
    row = lax.broadcasted_iota(jnp.int32, (n, n), 0)
    col = lax.broadcasted_iota(jnp.int32, (n, n), 1)
    shift = SOLVE_BLOCK.bit_length() - 1
    same = (row >> shift) == (col >> shift)
    es = [jnp.where(same, -A, 0.0) for A in As]
    offs = [jnp.where(same, 0.0, A) for A in As]
    ps = es
    span = 2
    while span < SOLVE_BLOCK:
        ps = [_bdot(p, p) for p in ps]
        es = [e + p + _bdot(e, p) for e, p in zip(es, ps)]
        span *= 2
    ys = [r + _bdot(e, r) for e, r in zip(es, rhss)]
    nblocks = L // SOLVE_BLOCK
    if nblocks > 1:
        powers = [[o + _bdot(e, o) for e, o in zip(es, offs)]]
        span = 2
        while span < nblocks:
            powers.append([_bdot(p, p) for p in powers[-1]])
            span *= 2
        for pw in reversed(powers[1:]):
            ys = [y + _bdot(p, y) for p, y in zip(pw, ys)]
        ys = [y - _bdot(p, y) for p, y in zip(powers[0], ys)]
    return ys


def _gdn_kernel(q_ref, k_ref, v_ref, z_ref, gb_ref, grow_ref, hist_ref, cw_ref, s0_ref, gn_ref, base_ref,
                o_ref, sfin_ref, conv_ref, s_scr, prev_scr, *, L, Hs, H):
    del base_ref
    c = pl.program_id(1)
    n = Hs * L
    tokd = H * HEAD_DIM
    lshift = L.bit_length() - 1

    @pl.when(c == 0)
    def _():
        s_scr[...] = s0_ref[0]
        prev_scr[...] = jnp.zeros_like(prev_scr)
        prev_scr[L - SUBLANES:L, :] = hist_ref[0]

    rowi = lax.broadcasted_iota(jnp.int32, (L, 1), 0)

    def conv_silu(t, x_ref):
        cs = slice(t * tokd, (t + 1) * tokd)
        x = x_ref[...]
        prev = prev_scr[:, cs]
        w = cw_ref[:, cs]
        y = None
        for s in range(CONV_W - 1, 0, -1):
            sh = jnp.where(rowi >= s, pltpu.roll(x, s, 0), pltpu.roll(prev, s, 0))
            term = sh * w[CONV_W - 1 - s:CONV_W - s]
            y = term if y is None else y + term
        y = y + x * w[CONV_W - 1:CONV_W]
        prev_scr[:, cs] = x
        return y * _sigmoid(y)

    yq = conv_silu(0, q_ref)
    yk = conv_silu(1, k_ref)
    yv = conv_silu(2, v_ref)

    @pl.when(c == pl.num_programs(1) - 1)
    def _():
        conv_ref[0] = prev_scr[L - SUBLANES:L, :]

    r2 = lax.broadcasted_iota(jnp.int32, (n, n), 0)
    c2 = lax.broadcasted_iota(jnp.int32, (n, n), 1)
    same = (r2 >> lshift) == (c2 >> lshift)
    incl = same & (r2 >= c2)
    strict = same & (r2 > c2)
    tril = jnp.where(incl, 1.0, 0.0).astype(BF16)
    triu = jnp.where(same & (r2 <= c2), 1.0, 0.0).astype(BF16)
    lane = lax.broadcasted_iota(jnp.int32, (n, LANES), 1)
    head_in_group = lax.broadcasted_iota(jnp.int32, (n, LANES), 0) >> lshift
    gb4 = jnp.concatenate([gb_ref[...]] * Hs, axis=0)
    gnorm = gn_ref[...]

    def stack(t, heads):
        return jnp.concatenate([t[:, h * HEAD_DIM:(h + 1) * HEAD_DIM] for h in heads], axis=0)

    groups = [[gi * Hs + j for j in range(Hs)] for gi in range(H // Hs)]
    headrows = [head_in_group + hs[0] for hs in groups]
    rsl = [slice(j * L, (j + 1) * L) for j in range(Hs)]

    def dot_split3(parts, m, left):
        return sum(_dot(m, p) if left else _dot(p, m) for p in parts)

    Gs = [jnp.sum(dot_split3(_split3(jnp.where(lane == hr, gb4, 0.0)), tril, True), axis=1, keepdims=True)
          for hr in headrows]
    Grs = [dot_split3(_split3(grow_ref[0, gi]), triu, False)[0:1] for gi in range(len(groups))]
    betas = [jnp.sum(jnp.where(lane == hr + H, gb4, 0.0), axis=1, keepdims=True) for hr in headrows]
    glasts = [[G[(j + 1) * L - 1:(j + 1) * L] for j in range(Hs)] for G in Gs]
    GLs = [jnp.concatenate([jnp.broadcast_to(g, (L, 1)) for g in gl], axis=0) for gl in glasts]

    def l2n(t):
        return t * lax.rsqrt(jnp.sum(t * t, axis=-1, keepdims=True) + EPS)

    qs = [l2n(stack(yq, hs)) * SCALE for hs in groups]
    ks = [l2n(stack(yk, hs)) for hs in groups]
    vs = [stack(yv, hs) for hs in groups]
    eGs = [jnp.exp(G) for G in Gs]
    decays = [jnp.exp(jnp.where(incl, G - Gr, -jnp.inf)) for G, Gr in zip(Gs, Grs)]
    kbs = [k.astype(BF16) for k in ks]
    kks = [_dot_nt(kb, kb) for kb in kbs]
    qks = [_dot_nt(q.astype(BF16), kb) for q, kb in zip(qs, kbs)]
    As = [jnp.where(strict, b * kk * d, 0.0) for b, kk, d in zip(betas, kks, decays)]
    rhss = [jnp.concatenate([b * v, (b * eG) * k], axis=-1) for b, v, eG, k in zip(betas, vs, eGs, ks)]
    sols = _unit_lower_solve(As, rhss, n, L)
    us = [s[:, :HEAD_DIM] for s in sols]
    ws = [s[:, HEAD_DIM:].astype(BF16) for s in sols]
    qes = [(q * eG).astype(BF16) for q, eG in zip(qs, eGs)]
    kds = [(k * jnp.exp(GL - G)).astype(BF16) for k, GL, G in zip(ks, GLs, Gs)]
    attns = [(qk * d).astype(BF16) for qk, d in zip(qks, decays)]

    sbs = [[s_scr[h].astype(BF16) for h in hs] for hs in groups]
    v_news = [jnp.concatenate([u[rs] - _dot(w[rs], sb) for rs, sb in zip(rsl, sbg)], axis=0).astype(BF16)
              for u, w, sbg in zip(us, ws, sbs)]
    outs = [jnp.concatenate([_dot(qe[rs], sb) for rs, sb in zip(rsl, sbg)], axis=0) + _dot(at, vn)
            for qe, sbg, at, vn in zip(qes, sbs, attns, v_news)]
    for hs, gl, kd, vn in zip(groups, glasts, kds, v_news):
        for j, h in enumerate(hs):
            s_scr[h] = jnp.exp(gl[j]) * s_scr[h] + _dot_tn(kd[rsl[j]], vn[rsl[j]])

    z = z_ref[...]
    for hs, out in zip(groups, outs):
        zz = stack(z, hs)
        o = (_rms(out) * gnorm * (zz * _sigmoid(zz))).astype(o_ref.dtype)
        for j, h in enumerate(hs):
            o_ref[:, h * HEAD_DIM:(h + 1) * HEAD_DIM] = o[rsl[j]]

    @pl.when(c == pl.num_programs(1) - 1)
    def _():
        sfin_ref[0] = s_scr[...]


def gdn_mix(proj, gb, hist8, cw8, s0, gnorm, base, *, row0, nseq, T, L, H):
    Hs = GDN_STACK
    NC = T // L
    NG = H // Hs
    n = Hs * L
    tokd = H * HEAD_DIM
    rb0 = row0 // L
    rows = nseq * T
    g_rows = gb[row0:row0 + rows, :H].reshape(nseq * NC, L, NG, Hs).transpose(0, 2, 3, 1)
    g_rows = jnp.pad(g_rows.reshape(nseq * NC, NG, 1, n), ((0, 0), (0, 0), (0, SUBLANES - 1), (0, 0)))

    def col(group):
        return lambda b, c: (rb0 + b * NC + c, group)

    base, base_spec, aliases = _row_range_output(base, proj.shape[0], tokd, 10)
    return pl.pallas_call(
        functools.partial(_gdn_kernel, L=L, Hs=Hs, H=H),
        out_shape=(jax.ShapeDtypeStruct((proj.shape[0], tokd), BF16),
                   jax.ShapeDtypeStruct((nseq, H, HEAD_DIM, HEAD_DIM), F32),
                   jax.ShapeDtypeStruct((nseq, SUBLANES, 3 * tokd), F32)),
        grid=(nseq, NC),
        in_specs=[pl.BlockSpec((L, tokd), col(0)), pl.BlockSpec((L, tokd), col(1)),
                  pl.BlockSpec((L, tokd), col(2)), pl.BlockSpec((L, tokd), col(3)),
                  pl.BlockSpec((L, LANES), lambda b, c: (rb0 + b * NC + c, 0)),
                  pl.BlockSpec((1, NG, SUBLANES, n), lambda b, c: (b * NC + c, 0, 0, 0)),
                  pl.BlockSpec((1, SUBLANES, 3 * tokd), lambda b, c: (b, 0, 0)),
                  pl.BlockSpec((SUBLANES, 3 * tokd), lambda b, c: (0, 0)),
                  pl.BlockSpec((1, H, HEAD_DIM, HEAD_DIM), lambda b, c: (b, 0, 0, 0)),
                  pl.BlockSpec((1, HEAD_DIM), lambda b, c: (0, 0)), base_spec],
        out_specs=(pl.BlockSpec((L, tokd), lambda b, c: (rb0 + b * NC + c, 0)),
                   pl.BlockSpec((1, H, HEAD_DIM, HEAD_DIM), lambda b, c: (b, 0, 0, 0)),
                   pl.BlockSpec((1, SUBLANES, 3 * tokd), lambda b, c: (b, 0, 0))),
        scratch_shapes=[pltpu.VMEM((H, HEAD_DIM, HEAD_DIM), F32),
                        pltpu.VMEM((L, 3 * tokd), F32)],
        input_output_aliases=aliases,
        compiler_params=_params(("parallel", "arbitrary")),
        name="gdn_mix",
    )(proj, proj, proj, proj, gb, g_rows, hist8, cw8, s0, gnorm.reshape(1, HEAD_DIM), base)


def _cumsum_kernel(x_ref, o_ref, carry_ref, *, tb):
    @pl.when(pl.program_id(0) == 0)
    def _():
        carry_ref[...] = jnp.zeros_like(carry_ref)

    r = lax.broadcasted_iota(jnp.int32, (tb, tb), 0)
    c = lax.broadcasted_iota(jnp.int32, (tb, tb), 1)
    triu = jnp.where(r <= c, 1.0, 0.0).astype(BF16)
    parts = _split3(x_ref[...])
    out = (_dot(parts[0], triu) + _dot(parts[1], triu) + _dot(parts[2], triu)) + carry_ref[...]
    o_ref[...] = out
    carry_ref[...] = out[:, tb - 1:tb]


def cumsum_lanes(x, *, tb):
    R, T = x.shape
    return pl.pallas_call(
        functools.partial(_cumsum_kernel, tb=tb),
        out_shape=jax.ShapeDtypeStruct((R, T), F32),
        grid=(T // tb,),
        in_specs=[pl.BlockSpec((R, tb), lambda i: (0, i))],
        out_specs=pl.BlockSpec((R, tb), lambda i: (0, i)),
        scratch_shapes=[pltpu.VMEM((R, 1), F32)],
        compiler_params=_params(("arbitrary",)),
        name="cumsum_lanes",
    )(x)


def _fox_prompt_kernel(qi_tab, ki_tab, q_ref, gate_ref, k_ref, v_ref, cq_ref, ck_ref, o_ref,
                       cq_scr, m_scr, acc_scr, *, t, hb, rsub):
    hg = pl.program_id(1)
    pair = pl.program_id(2)
    qi = qi_tab[pair]
    ki = ki_tab[pair]
    tr = t // rsub
    reps = t // LANES
    hsl = [slice(j * HEAD_DIM, (j + 1) * HEAD_DIM) for j in range(hb)]

    @pl.when(ki == 0)
    def _():
        lane = lax.broadcasted_iota(jnp.int32, cq_ref.shape, 1)
        for j in range(hb):
            col = jnp.sum(jnp.where(lane == hg * hb + j, cq_ref[...], 0.0), axis=1, keepdims=True) * LOG2E
            cq_scr[j] = jnp.broadcast_to(col, (t, LANES))
        m_scr[...] = jnp.full_like(m_scr, -jnp.inf)
        acc_scr[...] = jnp.zeros_like(acc_scr)

    def step(masked):
        probs = [(j, slice(r * tr, (r + 1) * tr), r * tr) for j in range(hb) for r in range(rsub)]
        ck2 = [ck_ref[0, j:j + 1, :] * LOG2E for j in range(hb)]
        v1 = [jnp.concatenate([v_ref[:, hsl[j]], jnp.ones((t, LANES), BF16)], axis=1) for j in range(hb)]
        ss = [_dot_nt(q_ref[rs, hsl[j]], k_ref[:, hsl[j]]) for j, rs, _ in probs]
        ss = [s * (SCALE * LOG2E) + (jnp.concatenate([cq_scr[j, rs]] * reps, axis=1) - ck2[j])
              for s, (j, rs, _) in zip(ss, probs)]
        if masked:
            row = lax.broadcasted_iota(jnp.int32, (tr, t), 0)
            col = lax.broadcasted_iota(jnp.int32, (tr, t), 1)
            ss = [jnp.where(col <= row + r0, s, -jnp.inf) for s, (_, _, r0) in zip(ss, probs)]
        m_olds = [m_scr[j, rs] for j, rs, _ in probs]
        m_news = [jnp.maximum(mo, jnp.max(s, axis=-1, keepdims=True)) for mo, s in zip(m_olds, ss)]
        ps = [jnp.exp2(s - jnp.concatenate([mn] * reps, axis=1)).astype(BF16) for s, mn in zip(ss, m_news)]
        pvs = [_dot(p, v1[j]) for p, (j, _, _) in zip(ps, probs)]
        for (j, rs, _), mo, mn, pv in zip(probs, m_olds, m_news, pvs):
            alpha = jnp.exp2(mo - mn)
            acc_scr[j, rs] = jnp.concatenate([alpha, alpha], axis=1) * acc_scr[j, rs] + pv
            m_scr[j, rs] = mn

    @pl.when(ki < qi)
    def _():
        step(False)

    @pl.when(ki == qi)
    def _():
        step(True)
        for j in range(hb):
            acc = acc_scr[j]
            o = acc[:, :HEAD_DIM] / acc[:, HEAD_DIM:HEAD_DIM + 1]
            o_ref[:, hsl[j]] = (o * _sigmoid(gate_ref[:, hsl[j]].astype(F32))).astype(o_ref.dtype)


def fox_prompt(proj, karr, varr, c_col, c_rows, *, nseq, T, H, t, hb, rsub):
    nb = T // t
    ng = H // hb
    W = hb * HEAD_DIM
    pairs = [(qi, ki) for qi in range(nb) for ki in range(qi + 1)]
    qi_tab = jnp.asarray([p[0] for p in pairs], jnp.int32)
    ki_tab = jnp.asarray([p[1] for p in pairs], jnp.int32)
    grid_spec = pltpu.PrefetchScalarGridSpec(
        num_scalar_prefetch=2,
        grid=(nseq, ng, len(pairs)),
        in_specs=[pl.BlockSpec((t, W), lambda b, g, p, qt, kt: (b * nb + qt[p], g)),
                  pl.BlockSpec((t, W), lambda b, g, p, qt, kt: (b * nb + qt[p], ng + g)),
                  pl.BlockSpec((t, W), lambda b, g, p, qt, kt: (b * nb + kt[p], g)),
                  pl.BlockSpec((t, W), lambda b, g, p, qt, kt: (b * nb + kt[p], g)),
                  pl.BlockSpec((t, H), lambda b, g, p, qt, kt: (b * nb + qt[p], 0)),
                  pl.BlockSpec((1, hb, t), lambda b, g, p, qt, kt: (b * ng + g, 0, kt[p]))],
        out_specs=pl.BlockSpec((t, W), lambda b, g, p, qt, kt: (b * nb + qt[p], g)),
        scratch_shapes=[pltpu.VMEM((hb, t, LANES), F32), pltpu.VMEM((hb, t, LANES), F32),
                        pltpu.VMEM((hb, t, 2 * HEAD_DIM), F32)])
    return pl.pallas_call(
        functools.partial(_fox_prompt_kernel, t=t, hb=hb, rsub=rsub),
        out_shape=jax.ShapeDtypeStruct((proj.shape[0], H * HEAD_DIM), BF16),
        grid_spec=grid_spec,
        compiler_params=_params(("parallel", "parallel", "arbitrary")),
        name="fox_prompt",
    )(qi_tab, ki_tab, proj, proj, karr, varr, c_col, c_rows.reshape(nseq * ng, hb, T))


def _fox_sample_kernel(q_ref, gate_ref, ck_ref, cv_ref, kn_ref, vn_ref, cq_ref, cc_ref, cn_ref, base_ref, o_ref,
                       m_scr, l_scr, acc_scr, *, H, T):
    del base_ref
    ki = pl.program_id(1)

    @pl.when(ki == 0)
    def _():
        m_scr[...] = jnp.full_like(m_scr, -jnp.inf)
        l_scr[...] = jnp.zeros_like(l_scr)
        acc_scr[...] = jnp.zeros_like(acc_scr)

    cq = cq_ref[...]

    sls = [slice(h * HEAD_DIM, (h + 1) * HEAD_DIM) for h in range(H)]

    def update(k_of, v_of, ck_of, mask):
        ss = [_dot_nt(q_ref[:, sl], k_of(h)) * SCALE + (cq[:, h:h + 1] - ck_of(h)) for h, sl in enumerate(sls)]
        if mask is not None:
            ss = [jnp.where(mask, s, -jnp.inf) for s in ss]
        m_olds = [m_scr[h] for h in range(H)]
        m_news = [jnp.maximum(mo, jnp.max(s, axis=-1, keepdims=True)) for mo, s in zip(m_olds, ss)]
        alphas = [jnp.exp(mo - mn) for mo, mn in zip(m_olds, m_news)]
        ps = [jnp.exp(s - mn) for s, mn in zip(ss, m_news)]
        pvs = [_dot(p.astype(BF16), v_of(h)) for h, p in enumerate(ps)]
        for h, sl in enumerate(sls):
            l_scr[h] = alphas[h] * l_scr[h] + jnp.sum(ps[h], axis=-1, keepdims=True)
            acc_scr[:, sl] = alphas[h] * acc_scr[:, sl] + pvs[h]
            m_scr[h] = m_news[h]

    update(lambda h: ck_ref[0, h].astype(BF16), lambda h: cv_ref[0, h].astype(BF16),
           lambda h: cc_ref[0, h:h + 1, :], None)

    @pl.when(ki == pl.num_programs(1) - 1)
    def _():
        causal = (lax.broadcasted_iota(jnp.int32, (T, T), 1) <= lax.broadcasted_iota(jnp.int32, (T, T), 0))
        update(lambda h: kn_ref[:, sls[h]], lambda h: vn_ref[:, sls[h]], lambda h: cn_ref[0, h:h + 1, :], causal)
        for h, sl in enumerate(sls):
            o = acc_scr[:, sl] / l_scr[h] * _sigmoid(gate_ref[:, sl].astype(F32))
            o_ref[:, sl] = o.astype(o_ref.dtype)


def fox_sample(proj, cache_k, cache_v, k_new, v_new, c_col, c_cache, c_new, base, *, row0, nseq, T, H, tk):
    P = cache_k.shape[2]
    W = H * HEAD_DIM
    rb0 = row0 // T
    cache_spec = pl.BlockSpec((1, H, tk, HEAD_DIM), lambda b, i: (b, 0, i, 0))
    base, base_spec, aliases = _row_range_output(base, proj.shape[0], W, 9)
    new_spec = pl.BlockSpec((T, W), lambda b, i: (b, 0))
    return pl.pallas_call(
        functools.partial(_fox_sample_kernel, H=H, T=T),
        out_shape=jax.ShapeDtypeStruct((proj.shape[0], W), BF16),
        grid=(nseq, P // tk),
        in_specs=[pl.BlockSpec((T, W), lambda b, i: (rb0 + b, 0)),
                  pl.BlockSpec((T, W), lambda b, i: (rb0 + b, 1)),
                  cache_spec, cache_spec, new_spec, new_spec,
                  pl.BlockSpec((T, H), lambda b, i: (b, 0)),
                  pl.BlockSpec((1, H, tk), lambda b, i: (b, 0, i)),
                  pl.BlockSpec((1, H, T), lambda b, i: (b, 0, 0)), base_spec],
        out_specs=pl.BlockSpec((T, W), lambda b, i: (rb0 + b, 0)),
        scratch_shapes=[pltpu.VMEM((H, T, 1), F32), pltpu.VMEM((H, T, 1), F32), pltpu.VMEM((T, W), F32)],
        input_output_aliases=aliases,
        compiler_params=_params(("parallel", "arbitrary")),
        name="fox_sample",
    )(proj, proj, cache_k, cache_v, k_new, v_new, c_col, c_cache, c_new, base)


def _pad_cols(a, width):
    return jnp.pad(a, ((0, 0), (0, width - a.shape[1])))


def kernel(x_prompt, x_sample, state_gdn, state_conv, cache_k, cache_v, cache_logf, cache_mem_k, cache_mem_v, mem_prompt, norm_mix_pre, norm_mix_post, norm_mlp_pre, norm_mlp_post, w_in_a, conv_w_a, a_log, dt_bias, gdn_norm, w_in_b, norm_kv, w_kvf, b_f, norm_mem, w_mem_kv, w_o, w_up, w_down):
    Bp, Tp, D = x_prompt.shape
    Bs, Ts, _ = x_sample.shape
    n_a = w_in_a.shape[0]
    depth = w_o.shape[0]
    H = a_log.shape[1]
    tokd = H * HEAD_DIM
    qkvd = 3 * tokd
    P = cache_k.shape[1]
    nmem = mem_prompt.shape[1]
    mh = cache_mem_k.shape[3]
    memd = mh * HEAD_DIM
    Mp, Ms = Bp * Tp, Bs * Ts
    M = Mp + Ms
    TM = next(t for t in (1024, 512, 256, 128, 64, 32) if M % t == 0)
    TMH = max(TM // 2, 32)
    HR = -(-H // SUBLANES) * SUBLANES

    x = jnp.concatenate([x_prompt.reshape(Mp, D), x_sample.reshape(Ms, D)], axis=0)

    mem_rows = mem_prompt.reshape(Bp * nmem, D)
    wkv = w_mem_kv.astype(BF16)
    pkv = jnp.stack([norm_matmul(mem_rows, norm_mem[l], wkv, layer=l, tm=min(TM, Bp * nmem), tn=memd)
                     for l in range(depth)])
    p_mem_k = pkv[:, :, :memd].reshape(depth, Bp, nmem, mh, HEAD_DIM)
    p_mem_v = pkv[:, :, memd:].reshape(depth, Bp, nmem, mh, HEAD_DIM)
    wo_all = w_o.astype(BF16)
    wup_all = w_up.astype(BF16)
    wdn_all = w_down.astype(BF16)
    wb_all = w_in_b.astype(BF16)

    def hist8(h):
        return jnp.pad(h, ((0, 0), (SUBLANES - (CONV_W - 1), 0), (0, 0)))

    cache_k2 = cache_k.transpose(0, 2, 1, 3)
    cache_v2 = cache_v.transpose(0, 2, 1, 3)

    p_gdn, s_gdn, p_conv, s_conv = [], [], [], []
    lane = jnp.arange(LANES)
    for l in range(depth):
        if l == n_a:
            wk = w_kvf[:, :tokd].astype(BF16)
            wv = w_kvf[:, tokd:2 * tokd].astype(BF16)
            wf = _pad_cols(w_kvf[:, 2 * tokd:], LANES).astype(BF16)
            kp2, p_k = kv_proj(x, norm_kv, wk, row0=0, nseq=Bp, T=Tp, tm=TMH, H=H)
            vp2, p_v = kv_proj(x, norm_kv, wv, row0=0, nseq=Bp, T=Tp, tm=TMH, H=H)
            ks2, s_k = kv_proj(x, norm_kv, wk, row0=Mp, nseq=Bs, T=Ts, tm=TMH, H=H)
            vs2, s_v = kv_proj(x, norm_kv, wv, row0=Mp, nseq=Bs, T=Ts, tm=TMH, H=H)
            pf = jnp.zeros((SUBLANES, LANES), F32).at[0, :H].set(b_f)
            logf = norm_matmul(x, norm_kv, wf[None], tm=TM, tn=LANES, epilogue=_ep_log_forget, p=pf)[:, :H]
            lf_p = logf[:Mp].reshape(Bp, Tp, H)
            lf_s = logf[Mp:].reshape(Bs, Ts, H)

            def rows_of(a):
                a = jnp.pad(a.transpose(0, 2, 1), ((0, 0), (0, HR - H), (0, 0)))
                return a.reshape(a.shape[0] * HR, a.shape[2])

            cp_rows = cumsum_lanes(rows_of(lf_p), tb=min(Tp, 512)).reshape(Bp, HR, Tp)[:, :H]
            cp_col = cp_rows.transpose(0, 2, 1).reshape(Mp, H)
            tot = P + Ts
            tot_pad = -(-tot // LANES) * LANES
            lf_all = jnp.pad(jnp.concatenate([cache_logf, lf_s], axis=1), ((0, 0), (0, tot_pad - tot), (0, 0)))
            cs_rows = cumsum_lanes(rows_of(lf_all), tb=LANES).reshape(Bs, HR, tot_pad)[:, :H]
            cs_cache = cs_rows[:, :, :P]
            cs_new = cs_rows[:, :, P:tot]
            cs_col = cs_new.transpose(0, 2, 1).reshape(Ms, H)

        if l < n_a:
            w = w_in_a[l]
            o1 = qkvd + tokd
            w_main = jnp.concatenate([w[:, :o1], w[:, o1 + 2 * H:]], axis=1).astype(BF16)[None]
            w_ab = _pad_cols(w[:, o1:o1 + 2 * H], LANES).astype(BF16)[None]
            proj = norm_matmul(x, norm_mix_pre[l], w_main, tm=TM, tn=_col_tile(w_main.shape[2]))
            pg = jnp.zeros((SUBLANES, LANES), F32)
            pg = pg.at[0, :H].set(a_log[l]).at[1, :H].set(dt_bias[l]).at[2].set((lane < H).astype(F32))
            gb = norm_matmul(x, norm_mix_pre[l], w_ab, tm=TM, tn=LANES, epilogue=_ep_gdn_gates, p=pg)
            cw8 = jnp.pad(conv_w_a[l], ((0, SUBLANES - CONV_W), (0, 0)))
            tok, sp, cp = gdn_mix(proj, gb, hist8(jnp.zeros((Bp, CONV_W - 1, qkvd), F32)), cw8,
                                  jnp.zeros((Bp, H, HEAD_DIM, HEAD_DIM), F32), gdn_norm[l], None,
                                  row0=0, nseq=Bp, T=Tp, L=min(Tp, GDN_CHUNK), H=H)
            tok, ss, cs = gdn_mix(proj, gb, hist8(state_conv[l]), cw8, state_gdn[l], gdn_norm[l], tok,
                                  row0=Mp, nseq=Bs, T=Ts, L=min(Ts, GDN_CHUNK), H=H)
            p_gdn.append(sp)
            s_gdn.append(ss)
            p_conv.append(cp[:, SUBLANES - (CONV_W - 1):])
            s_conv.append(cs[:, SUBLANES - (CONV_W - 1):])
            mq_block = (qkvd + tokd) // memd
        else:
            proj = norm_matmul(x, norm_mix_pre[l], wb_all, layer=l - n_a, tm=TM, tn=_col_tile(wb_all.shape[2]),
                               out_dtype=BF16)
            tfox = min(Tp, 1024)
            tok = fox_prompt(proj, kp2, vp2, cp_col, cp_rows, nseq=Bp, T=Tp, H=H, t=tfox,
                             hb=2, rsub=max(tfox // 512, 1))
            tok = fox_sample(proj, cache_k2, cache_v2, ks2, vs2, cs_col, cs_cache, cs_new, tok,
                             row0=Mp, nseq=Bs, T=Ts, H=H, tk=min(P, 1024))
            mq_block = 2 * tokd // memd
        mem_o = mem_attend(proj, p_mem_k, p_mem_v, l, None, row0=0, rows_per_seq=Tp, tm=min(Tp, 512),
                           col_block=mq_block)
        mem_o = mem_attend(proj, cache_mem_k, cache_mem_v, l, mem_o, row0=Mp, rows_per_seq=Ts, tm=Ts,
                           col_block=mq_block)
        x = out_proj(tok, mem_o, x, wo_all, l, norm_mix_post[l], tm=TMH)
        x = mlp(x, norm_mlp_pre[l], norm_mlp_post[l], wup_all, wdn_all, l, tm=TMH, tf=min(wup_all.shape[2], 1024))

    y_prompt = x[:Mp].reshape(Bp, Tp, D)
    y_sample = x[Mp:].reshape(Bs, Ts, D)
    return (y_prompt, y_sample, jnp.stack(p_gdn), jnp.stack(p_conv),
            p_k.transpose(0, 2, 1, 3), p_v.transpose(0, 2, 1, 3), lf_p, p_mem_k, p_mem_v,
            jnp.stack(s_gdn), jnp.stack(s_conv),
            s_k.transpose(0, 2, 1, 3), s_v.transpose(0, 2, 1, 3), lf_s)
```

```python
import functools
import math

import jax
import jax.numpy as jnp
from jax import lax
from jax.experimental import pallas as pl
from jax.experimental.pallas import tpu as pltpu

F32 = jnp.float32
BF16 = jnp.bfloat16
EPS = 1e-6
HEAD_DIM = 128
LANES = 128
SUBLANES = 8
SCALE = HEAD_DIM ** -0.5
LOG2E = math.log2(math.e)
CONV_W = 4
GDN_CHUNK = 64
GDN_STACK = 2
SOLVE_BLOCK = 16
HI = lax.Precision.HIGHEST
VMEM_LIMIT = 56 * 1024 * 1024


def _params(sem, vmem=VMEM_LIMIT):
    return pltpu.CompilerParams(dimension_semantics=sem, vmem_limit_bytes=vmem)


def _rms(x):
    return x * lax.rsqrt(jnp.mean(x * x, axis=-1, keepdims=True) + EPS)


def _sigmoid(x):
    return 1.0 / (1.0 + jnp.exp(-x))


def _softplus(x):
    return jnp.maximum(x, 0.0) + jnp.log1p(jnp.exp(-jnp.abs(x)))


def _dot(a, b, precision=None):
    return jnp.dot(a, b, preferred_element_type=F32, precision=precision)


def _dot_tn(a, b):
    return lax.dot_general(a, b, (((0,), (0,)), ((), ())), preferred_element_type=F32)


def _dot_nt(a, b):
    return lax.dot_general(a, b, (((1,), (1,)), ((), ())), preferred_element_type=F32)


def _split2(x):
    hi = x.astype(BF16)
    return hi, (x - hi.astype(F32)).astype(BF16)


def _split3(x):
    hi = x.astype(BF16)
    r = x - hi.astype(F32)
    mid = r.astype(BF16)
    return hi, mid, (r - mid.astype(F32)).astype(BF16)


def _dot3(a, b):
    return _dot(a[0], b[0]) + (_dot(a[0], b[1]) + _dot(a[1], b[0]))


def _ep_none(acc, p):
    return acc


def _ep_gdn_gates(acc, p):
    g = -jnp.exp(p[0:1]) * _softplus(acc + p[1:2])
    return jnp.where(p[2:3] > 0.5, g, _sigmoid(acc))


def _ep_log_forget(acc, p):
    return -_softplus(-(acc + p[0:1]))


def _norm_matmul_kernel(x_ref, g_ref, w_ref, p_ref, o_ref, h_ref, *, epilogue):
    @pl.when(pl.program_id(1) == 0)
    def _():
        h_ref[...] = (_rms(x_ref[...]) * g_ref[...]).astype(BF16)

    acc = _dot(h_ref[...], w_ref[...])
    o_ref[...] = epilogue(acc, p_ref[...]).astype(o_ref.dtype)


def _col_tile(n, cap=1792):
    return max(c for c in range(LANES, min(n, cap) + 1, LANES) if n % c == 0)


def norm_matmul(x, g, w, *, tm, tn, layer=0, out_dtype=F32, epilogue=_ep_none, p=None):
    M, K = x.shape
    N = w.shape[2]
    if p is None:
        p = jnp.zeros((SUBLANES, tn), F32)
    return pl.pallas_call(
        functools.partial(_norm_matmul_kernel, epilogue=epilogue),
        out_shape=jax.ShapeDtypeStruct((M, N), out_dtype),
        grid=(M // tm, N // tn),
        in_specs=[pl.BlockSpec((tm, K), lambda i, j: (i, 0)),
                  pl.BlockSpec((1, K), lambda i, j: (0, 0)),
                  pl.BlockSpec((None, K, tn), lambda i, j: (layer, 0, j)),
                  pl.BlockSpec((SUBLANES, tn), lambda i, j: (0, 0))],
        out_specs=pl.BlockSpec((tm, tn), lambda i, j: (i, j)),
        scratch_shapes=[pltpu.VMEM((tm, K), BF16)],
        compiler_params=_params(("parallel", "arbitrary")),
        name="norm_matmul",
    )(x, g.reshape(1, K), w, p)


def _kv_proj_kernel(x_ref, g_ref, w_ref, o2_ref, o4_ref, *, H, ns, tt):
    h = (_rms(x_ref[...]) * g_ref[...]).astype(BF16)
    acc = _dot(h, w_ref[...])
    o2_ref[...] = acc.astype(o2_ref.dtype)
    for s in range(ns):
        for hh in range(H):
            o4_ref[s, hh] = acc[s * tt:(s + 1) * tt, hh * HEAD_DIM:(hh + 1) * HEAD_DIM]


def kv_proj(x, g, w, *, row0, nseq, T, tm, H):
    K = x.shape[1]
    N = w.shape[1]
    rows = nseq * T
    rb0 = row0 // tm
    ns, tt = (tm // T, T) if tm >= T else (1, tm)
    parts = T // tt
    return pl.pallas_call(
        functools.partial(_kv_proj_kernel, H=H, ns=ns, tt=tt),
        out_shape=(jax.ShapeDtypeStruct((rows, N), BF16),
                   jax.ShapeDtypeStruct((nseq, H, T, HEAD_DIM), F32)),
        grid=(rows // tm,),
        in_specs=[pl.BlockSpec((tm, K), lambda i: (rb0 + i, 0)),
                  pl.BlockSpec((1, K), lambda i: (0, 0)),
                  pl.BlockSpec((K, N), lambda i: (0, 0))],
        out_specs=(pl.BlockSpec((tm, N), lambda i: (i, 0)),
                   pl.BlockSpec((ns, H, tt, HEAD_DIM), lambda i: (i // parts, 0, i % parts, 0))),
        compiler_params=_params(("parallel",)),
        name="kv_proj",
    )(x, g.reshape(1, K), w)


def _out_proj_kernel(tok_ref, mem_ref, x_ref, wa_ref, wb_ref, g_ref, o_ref):
    mix = _dot(tok_ref[...], wa_ref[...]) + _dot(mem_ref[...], wb_ref[...])
    o_ref[...] = x_ref[...] + _rms(mix) * g_ref[...]


def out_proj(tok, mem, x, w, layer, g, *, tm):
    M, D = x.shape
    Ka, Kb = tok.shape[1], mem.shape[1]
    assert Ka % Kb == 0
    return pl.pallas_call(
        _out_proj_kernel,
        out_shape=jax.ShapeDtypeStruct((M, D), F32),
        grid=(M // tm,),
        in_specs=[pl.BlockSpec((tm, Ka), lambda i: (i, 0)),
                  pl.BlockSpec((tm, Kb), lambda i: (i, 0)),
                  pl.BlockSpec((tm, D), lambda i: (i, 0)),
                  pl.BlockSpec((None, Ka, D), lambda i: (layer, 0, 0)),
                  pl.BlockSpec((None, Kb, D), lambda i: (layer, Ka // Kb, 0)),
                  pl.BlockSpec((1, D), lambda i: (0, 0))],
        out_specs=pl.BlockSpec((tm, D), lambda i: (i, 0)),
        compiler_params=_params(("parallel",)),
        name="out_proj",
    )(tok, mem, x, w, w, g.reshape(1, D))


def _mlp_kernel(x_ref, gpre_ref, gpost_ref, wup_ref, wdn_ref, o_ref, h_ref, acc_ref):
    j = pl.program_id(1)

    @pl.when(j == 0)
    def _():
        h_ref[...] = (_rms(x_ref[...]) * gpre_ref[...]).astype(BF16)
        acc_ref[...] = jnp.zeros_like(acc_ref)

    up = _dot(h_ref[...], wup_ref[...])
    act = jnp.square(jnp.maximum(up, 0.0)).astype(BF16)
    acc_ref[...] += _dot(act, wdn_ref[...])

    @pl.when(j == pl.num_programs(1) - 1)
    def _():
        o_ref[...] = x_ref[...] + _rms(acc_ref[...]) * gpost_ref[...]


def mlp(x, gpre, gpost, wup, wdn, layer, *, tm, tf):
    M, D = x.shape
    FF = wup.shape[2]
    return pl.pallas_call(
        _mlp_kernel,
        out_shape=jax.ShapeDtypeStruct((M, D), F32),
        grid=(M // tm, FF // tf),
        in_specs=[pl.BlockSpec((tm, D), lambda i, j: (i, 0)),
                  pl.BlockSpec((1, D), lambda i, j: (0, 0)),
                  pl.BlockSpec((1, D), lambda i, j: (0, 0)),
                  pl.BlockSpec((None, D, tf), lambda i, j: (layer, 0, j)),
                  pl.BlockSpec((None, tf, D), lambda i, j: (layer, j, 0))],
        out_specs=pl.BlockSpec((tm, D), lambda i, j: (i, 0)),
        scratch_shapes=[pltpu.VMEM((tm, D), BF16), pltpu.VMEM((tm, D), F32)],
        compiler_params=_params(("parallel", "arbitrary")),
        name="mlp",
    )(x, gpre.reshape(1, D), gpost.reshape(1, D), wup, wdn)


def _row_range_output(base, total_rows, width, n_inputs):
    spec = pl.BlockSpec(memory_space=pl.ANY)
    if base is None:
        return jnp.zeros((SUBLANES, LANES), BF16), spec, {}
    assert base.shape == (total_rows, width) and base.dtype == BF16
    return base, spec, {n_inputs: 0}


def _mem_attn_kernel(q_ref, mk_ref, mv_ref, base_ref, o_ref, *, heads, sb, tm):
    del base_ref
    probs = [(s, slice(s * tm, (s + 1) * tm), h, slice(h * HEAD_DIM, (h + 1) * HEAD_DIM))
             for s in range(sb) for h in range(heads)]
    ss = [_dot_nt(q_ref[rs, sl].astype(BF16), mk_ref[0, s, :, h, :].astype(BF16)) * SCALE
          for s, rs, h, sl in probs]
    es = [jnp.exp(s - jnp.max(s, axis=-1, keepdims=True)) for s in ss]
    ps = [(e / jnp.sum(e, axis=-1, keepdims=True)).astype(BF16) for e in es]
    outs = [_dot(p, mv_ref[0, s, :, h, :].astype(BF16)) for p, (s, _, h, _) in zip(ps, probs)]
    for o, (_, rs, _, sl) in zip(outs, probs):
        o_ref[rs, sl] = o.astype(o_ref.dtype)


def mem_attend(proj, mk, mv, layer, base, *, row0, rows_per_seq, tm, sb, col_block):
    _, nseq, nmem, heads, _ = mk.shape
    assert sb == 1 or tm == rows_per_seq
    W = heads * HEAD_DIM
    nt = rows_per_seq // tm
    rb0 = row0 // (sb * tm)
    mem_spec = pl.BlockSpec((1, sb, nmem, heads, HEAD_DIM), lambda b, i: (layer, b, 0, 0, 0))
    base, base_spec, aliases = _row_range_output(base, proj.shape[0], W, 3)
    return pl.pallas_call(
        functools.partial(_mem_attn_kernel, heads=heads, sb=sb, tm=tm),
        out_shape=jax.ShapeDtypeStruct((proj.shape[0], W), BF16),
        grid=(nseq // sb, nt),
        in_specs=[pl.BlockSpec((sb * tm, W), lambda b, i: (rb0 + b * nt + i, col_block)), mem_spec, mem_spec,
                  base_spec],
        out_specs=pl.BlockSpec((sb * tm, W), lambda b, i: (rb0 + b * nt + i, 0)),
        input_output_aliases=aliases,
        compiler_params=_params(("parallel", "parallel")),
        name="mem_attend",
    )(proj, mk, mv, base)


def _bdot(a, b):
    return _dot(a.astype(BF16), b.astype(BF16))


def _unit_lower_solve(As, rhss, n, L):
    row = lax.broadcasted_iota(jnp.int32, (n, n), 0)
    col = lax.broadcasted_iota(jnp.int32, (n, n), 1)
    shift = SOLVE_BLOCK.bit_length() - 1
    same = (row >> shift) == (col >> shift)
    es = [jnp.where(same, -A, 0.0) for A in As]
    offs = [jnp.where(same, 0.0, A) for A in As]
    ps = es
    span = 2
    while span < SOLVE_BLOCK:
        ps = [_bdot(p, p) for p in ps]
        es = [e + p + _bdot(e, p) for e, p in zip(es, ps)]
        span *= 2
    ys = [r + _bdot(e, r) for e, r in zip(es, rhss)]
    nblocks = L // SOLVE_BLOCK
    if nblocks > 1:
        powers = [[o + _bdot(e, o) for e, o in zip(es, offs)]]
        span = 2
        while span < nblocks:
            powers.append([_bdot(p, p) for p in powers[-1]])
            span *= 2
        for pw in reversed(powers[1:]):
            ys = [y + _bdot(p, y) for p, y in zip(pw, ys)]
        ys = [y - _bdot(p, y) for p, y in zip(powers[0], ys)]
    return ys


def _gdn_kernel(q_ref, k_ref, v_ref, z_ref, gb_ref, hist_ref, cw_ref, s0_ref, gn_ref, base_ref,
                o_ref, sfin_ref, conv_ref, s_scr, prev_scr, *, L, Hs, H):
    del base_ref
    c = pl.program_id(1)
    n = Hs * L
    tokd = H * HEAD_DIM
    lshift = L.bit_length() - 1

    @pl.when(c == 0)
    def _():
        s_scr[...] = s0_ref[0]
        prev_scr[...] = hist_ref[0]

    row8 = lax.broadcasted_iota(jnp.int32, (SUBLANES, 1), 0)

    def conv_silu(t, x_ref):
        cs = slice(t * tokd, (t + 1) * tokd)
        x = x_ref[...]
        prev = prev_scr[:, cs]
        w = cw_ref[:, cs]
        y = None
        for s in range(CONV_W - 1, 0, -1):
            sh = pltpu.roll(x, s, 0)
            top = jnp.where(row8 >= s, sh[:SUBLANES], pltpu.roll(prev, s, 0))
            term = jnp.concatenate([top, sh[SUBLANES:]], axis=0) * w[CONV_W - 1 - s:CONV_W - s]
            y = term if y is None else y + term
        y = y + x * w[CONV_W - 1:CONV_W]
        prev_scr[:, cs] = x[L - SUBLANES:]
        return y * _sigmoid(y)

    yq = conv_silu(0, q_ref)
    yk = conv_silu(1, k_ref)
    yv = conv_silu(2, v_ref)

    @pl.when(c == pl.num_programs(1) - 1)
    def _():
        conv_ref[0] = prev_scr[...]

    r2 = lax.broadcasted_iota(jnp.int32, (n, n), 0)
    c2 = lax.broadcasted_iota(jnp.int32, (n, n), 1)
    same = (r2 >> lshift) == (c2 >> lshift)
    incl = same & (r2 >= c2)
    strict = same & (r2 > c2)
    gnorm = gn_ref[...]

    def stack(t, heads):
        return jnp.concatenate([t[:, h * HEAD_DIM:(h + 1) * HEAD_DIM] for h in heads], axis=0)

    groups = [[gi * Hs + j for j in range(Hs)] for gi in range(H // Hs)]
    rsl = [slice(j * L, (j + 1) * L) for j in range(Hs)]

    gb = gb_ref[...]
    rl = lax.broadcasted_iota(jnp.int32, (L, L), 0)
    cl = lax.broadcasted_iota(jnp.int32, (L, L), 1)
    tril = jnp.where(rl >= cl, 1.0, 0.0).astype(BF16)
    gcum = sum(_dot(tril, part) for part in _split3(gb))
    gcum_t = jnp.concatenate([gcum, jnp.zeros((LANES - L, LANES), F32)], axis=0).T if L < LANES else gcum.T

    def col(tile, heads, off=0):
        return jnp.concatenate([tile[:, off + h:off + h + 1] for h in heads], axis=0)

    Gs = [col(gcum, hs) for hs in groups]
    Grs = [jnp.concatenate([gcum_t[h:h + 1, :L] for h in hs], axis=1) for hs in groups]
    betas = [col(gb, hs, H) for hs in groups]
    glasts = [[gcum[L - 1:L, h:h + 1] for h in hs] for hs in groups]
    GLs = [jnp.concatenate([jnp.broadcast_to(g, (L, 1)) for g in gl], axis=0) for gl in glasts]

    def l2n(t):
        return t * lax.rsqrt(jnp.sum(t * t, axis=-1, keepdims=True) + EPS)

    qs = [l2n(stack(yq, hs)) * SCALE for hs in groups]
    ks = [l2n(stack(yk, hs)) for hs in groups]
    vs = [stack(yv, hs) for hs in groups]
    eGs = [jnp.exp(G) for G in Gs]
    decays = [jnp.exp(jnp.where(incl, G - Gr, -jnp.inf)) for G, Gr in zip(Gs, Grs)]
    kbs = [k.astype(BF16) for k in ks]
    kks = [_dot_nt(kb, kb) for kb in kbs]
    qks = [_dot_nt(q.astype(BF16), kb) for q, kb in zip(qs, kbs)]
    As = [jnp.where(strict, b * kk * d, 0.0) for b, kk, d in zip(betas, kks, decays)]
    rhss = [jnp.concatenate([b * v, (b * eG) * k], axis=-1) for b, v, eG, k in zip(betas, vs, eGs, ks)]
    sols = _unit_lower_solve(As, rhss, n, L)
    us = [s[:, :HEAD_DIM] for s in sols]
    ws = [s[:, HEAD_DIM:].astype(BF16) for s in sols]
    qes = [(q * eG).astype(BF16) for q, eG in zip(qs, eGs)]
    kds = [(k * jnp.exp(GL - G)).astype(BF16) for k, GL, G in zip(ks, GLs, Gs)]
    attns = [(qk * d).astype(BF16) for qk, d in zip(qks, decays)]

    sbs = [[s_scr[h].astype(BF16) for h in hs] for hs in groups]
    v_news = [jnp.concatenate([u[rs] - _dot(w[rs], sb) for rs, sb in zip(rsl, sbg)], axis=0).astype(BF16)
              for u, w, sbg in zip(us, ws, sbs)]
    outs = [jnp.concatenate([_dot(qe[rs], sb) for rs, sb in zip(rsl, sbg)], axis=0) + _dot(at, vn)
            for qe, sbg, at, vn in zip(qes, sbs, attns, v_news)]
    for hs, gl, kd, vn in zip(groups, glasts, kds, v_news):
        for j, h in enumerate(hs):
            s_scr[h] = jnp.exp(gl[j]) * s_scr[h] + _dot_tn(kd[rsl[j]], vn[rsl[j]])

    z = z_ref[...]
    for hs, out in zip(groups, outs):
        zz = stack(z, hs)
        o = (_rms(out) * gnorm * (zz * _sigmoid(zz))).astype(o_ref.dtype)
        for j, h in enumerate(hs):
            o_ref[:, h * HEAD_DIM:(h + 1) * HEAD_DIM] = o[rsl[j]]

    @pl.when(c == pl.num_programs(1) - 1)
    def _():
        sfin_ref[0] = s_scr[...]


def gdn_mix(proj, gb, hist8, cw8, s0, gnorm, base, *, row0, nseq, T, L, H):
    NC = T // L
    tokd = H * HEAD_DIM
    rb0 = row0 // L

    def col(group):
        return lambda b, c: (rb0 + b * NC + c, group)

    base, base_spec, aliases = _row_range_output(base, proj.shape[0], tokd, 9)
    return pl.pallas_call(
        functools.partial(_gdn_kernel, L=L, Hs=GDN_STACK, H=H),
        out_shape=(jax.ShapeDtypeStruct((proj.shape[0], tokd), BF16),
                   jax.ShapeDtypeStruct((nseq, H, HEAD_DIM, HEAD_DIM), F32),
                   jax.ShapeDtypeStruct((nseq, SUBLANES, 3 * tokd), F32)),
        grid=(nseq, NC),
        in_specs=[pl.BlockSpec((L, tokd), col(0)), pl.BlockSpec((L, tokd), col(1)),
                  pl.BlockSpec((L, tokd), col(2)), pl.BlockSpec((L, tokd), col(3)),
                  pl.BlockSpec((L, LANES), lambda b, c: (rb0 + b * NC + c, 0)),
                  pl.BlockSpec((1, SUBLANES, 3 * tokd), lambda b, c: (b, 0, 0)),
                  pl.BlockSpec((SUBLANES, 3 * tokd), lambda b, c: (0, 0)),
                  pl.BlockSpec((1, H, HEAD_DIM, HEAD_DIM), lambda b, c: (b, 0, 0, 0)),
                  pl.BlockSpec((1, HEAD_DIM), lambda b, c: (0, 0)), base_spec],
        out_specs=(pl.BlockSpec((L, tokd), lambda b, c: (rb0 + b * NC + c, 0)),
                   pl.BlockSpec((1, H, HEAD_DIM, HEAD_DIM), lambda b, c: (b, 0, 0, 0)),
                   pl.BlockSpec((1, SUBLANES, 3 * tokd), lambda b, c: (b, 0, 0))),
        scratch_shapes=[pltpu.VMEM((H, HEAD_DIM, HEAD_DIM), F32),
                        pltpu.VMEM((SUBLANES, 3 * tokd), F32)],
        input_output_aliases=aliases,
        compiler_params=_params(("parallel", "arbitrary")),
        name="gdn_mix",
    )(proj, proj, proj, proj, gb, hist8, cw8, s0, gnorm.reshape(1, HEAD_DIM), base)


def _cumsum_kernel(x_ref, o_ref, carry_ref, *, tb):
    @pl.when(pl.program_id(0) == 0)
    def _():
        carry_ref[...] = jnp.zeros_like(carry_ref)

    r = lax.broadcasted_iota(jnp.int32, (tb, tb), 0)
    c = lax.broadcasted_iota(jnp.int32, (tb, tb), 1)
    triu = jnp.where(r <= c, 1.0, 0.0).astype(BF16)
    parts = _split3(x_ref[...])
    out = (_dot(parts[0], triu) + _dot(parts[1], triu) + _dot(parts[2], triu)) + carry_ref[...]
    o_ref[...] = out
    carry_ref[...] = out[:, tb - 1:tb]


def cumsum_lanes(x, *, tb):
    R, T = x.shape
    return pl.pallas_call(
        functools.partial(_cumsum_kernel, tb=tb),
        out_shape=jax.ShapeDtypeStruct((R, T), F32),
        grid=(T // tb,),
        in_specs=[pl.BlockSpec((R, tb), lambda i: (0, i))],
        out_specs=pl.BlockSpec((R, tb), lambda i: (0, i)),
        scratch_shapes=[pltpu.VMEM((R, 1), F32)],
        compiler_params=_params(("arbitrary",)),
        name="cumsum_lanes",
    )(x)


def _fox_prompt_kernel(qi_tab, ki_tab, q_ref, gate_ref, k_ref, v_ref, cq_ref, ck_ref, o_ref,
                       cq_scr, m_scr, acc_scr, *, t, hb, rsub):
    hg = pl.program_id(1)
    pair = pl.program_id(2)
    qi = qi_tab[pair]
    ki = ki_tab[pair]
    tr = t // rsub
    hsl = [slice(j * HEAD_DIM, (j + 1) * HEAD_DIM) for j in range(hb)]

    @pl.when(ki == 0)
    def _():
        lane = lax.broadcasted_iota(jnp.int32, cq_ref.shape, 1)
        for j in range(hb):
            col = jnp.sum(jnp.where(lane == hg * hb + j, cq_ref[...], 0.0), axis=1, keepdims=True) * LOG2E
            cq_scr[j] = jnp.broadcast_to(col, (t, LANES))
        m_scr[...] = jnp.full_like(m_scr, -jnp.inf)
        acc_scr[...] = jnp.zeros_like(acc_scr)

    def step(masked):
        probs = [(j, slice(r * tr, (r + 1) * tr), r * tr, (r + 1) * tr if masked else t)
                 for j in range(hb) for r in range(rsub)]
        ck2 = [ck_ref[0, j:j + 1, :] * LOG2E for j in range(hb)]
        v1 = [jnp.concatenate([v_ref[:, hsl[j]], jnp.ones((t, LANES), BF16)], axis=1) for j in range(hb)]
        ss = [_dot_nt(q_ref[rs, hsl[j]], k_ref[:kc, hsl[j]]) for j, rs, _, kc in probs]
        ss = [s * (SCALE * LOG2E) + (jnp.concatenate([cq_scr[j, rs]] * (kc // LANES), axis=1) - ck2[j][:, :kc])
              for s, (j, rs, _, kc) in zip(ss, probs)]
        if masked:
            ss = [jnp.where(lax.broadcasted_iota(jnp.int32, (tr, kc), 1)
                            <= lax.broadcasted_iota(jnp.int32, (tr, kc), 0) + r0, s, -jnp.inf)
                  for s, (_, _, r0, kc) in zip(ss, probs)]
        m_olds = [m_scr[j, rs] for j, rs, _, _ in probs]
        m_news = [jnp.maximum(mo, jnp.max(s, axis=-1, keepdims=True)) for mo, s in zip(m_olds, ss)]
        ps = [jnp.exp2(s - jnp.concatenate([mn] * (kc // LANES), axis=1)).astype(BF16)
              for s, mn, (_, _, _, kc) in zip(ss, m_news, probs)]
        pvs = [_dot(p, v1[j][:kc]) for p, (j, _, _, kc) in zip(ps, probs)]
        for (j, rs, _, _), mo, mn, pv in zip(probs, m_olds, m_news, pvs):
            alpha = jnp.exp2(mo - mn)
            acc_scr[j, rs] = jnp.concatenate([alpha, alpha], axis=1) * acc_scr[j, rs] + pv
            m_scr[j, rs] = mn

    @pl.when(ki < qi)
    def _():
        step(False)

    @pl.when(ki == qi)
    def _():
        step(True)
        for j in range(hb):
            acc = acc_scr[j]
            o = acc[:, :HEAD_DIM] / acc[:, HEAD_DIM:HEAD_DIM + 1]
            o_ref[:, hsl[j]] = (o * _sigmoid(gate_ref[:, hsl[j]].astype(F32))).astype(o_ref.dtype)


def fox_prompt(proj, karr, varr, c_col, c_rows, *, nseq, T, H, t, hb, rsub):
    nb = T // t
    ng = H // hb
    W = hb * HEAD_DIM
    pairs = [(qi, ki) for qi in range(nb) for ki in range(qi + 1)]
    qi_tab = jnp.asarray([p[0] for p in pairs], jnp.int32)
    ki_tab = jnp.asarray([p[1] for p in pairs], jnp.int32)
    grid_spec = pltpu.PrefetchScalarGridSpec(
        num_scalar_prefetch=2,
        grid=(nseq, ng, len(pairs)),
        in_specs=[pl.BlockSpec((t, W), lambda b, g, p, qt, kt: (b * nb + qt[p], g)),
                  pl.BlockSpec((t, W), lambda b, g, p, qt, kt: (b * nb + qt[p], ng + g)),
                  pl.BlockSpec((t, W), lambda b, g, p, qt, kt: (b * nb + kt[p], g)),
                  pl.BlockSpec((t, W), lambda b, g, p, qt, kt: (b * nb + kt[p], g)),
                  pl.BlockSpec((t, H), lambda b, g, p, qt, kt: (b * nb + qt[p], 0)),
                  pl.BlockSpec((1, hb, t), lambda b, g, p, qt, kt: (b * ng + g, 0, kt[p]))],
        out_specs=pl.BlockSpec((t, W), lambda b, g, p, qt, kt: (b * nb + qt[p], g)),
        scratch_shapes=[pltpu.VMEM((hb, t, LANES), F32), pltpu.VMEM((hb, t, LANES), F32),
                        pltpu.VMEM((hb, t, 2 * HEAD_DIM), F32)])
    return pl.pallas_call(
        functools.partial(_fox_prompt_kernel, t=t, hb=hb, rsub=rsub),
        out_shape=jax.ShapeDtypeStruct((proj.shape[0], H * HEAD_DIM), BF16),
        grid_spec=grid_spec,
        compiler_params=_params(("parallel", "parallel", "arbitrary")),
        name="fox_prompt",
    )(qi_tab, ki_tab, proj, proj, karr, varr, c_col, c_rows.reshape(nseq * ng, hb, T))


def _fox_sample_kernel(q_ref, gate_ref, ck_ref, cv_ref, kn_ref, vn_ref, cq_ref, cc_ref, cn_ref, base_ref, o_ref,
                       m_scr, l_scr, acc_scr, *, H, T):
    del base_ref
    ki = pl.program_id(1)

    @pl.when(ki == 0)
    def _():
        m_scr[...] = jnp.full_like(m_scr, -jnp.inf)
        l_scr[...] = jnp.zeros_like(l_scr)
        acc_scr[...] = jnp.zeros_like(acc_scr)

    cq = cq_ref[...]

    sls = [slice(h * HEAD_DIM, (h + 1) * HEAD_DIM) for h in range(H)]

    def update(k_of, v_of, ck_of, mask):
        ss = [_dot_nt(q_ref[:, sl], k_of(h)) * SCALE + (cq[:, h:h + 1] - ck_of(h)) for h, sl in enumerate(sls)]
        if mask is not None:
            ss = [jnp.where(mask, s, -jnp.inf) for s in ss]
        m_olds = [m_scr[h] for h in range(H)]
        m_news = [jnp.maximum(mo, jnp.max(s, axis=-1, keepdims=True)) for mo, s in zip(m_olds, ss)]
        alphas = [jnp.exp(mo - mn) for mo, mn in zip(m_olds, m_news)]
        ps = [jnp.exp(s - mn) for s, mn in zip(ss, m_news)]
        pvs = [_dot(p.astype(BF16), v_of(h)) for h, p in enumerate(ps)]
        for h, sl in enumerate(sls):
            l_scr[h] = alphas[h] * l_scr[h] + jnp.sum(ps[h], axis=-1, keepdims=True)
            acc_scr[:, sl] = alphas[h] * acc_scr[:, sl] + pvs[h]
            m_scr[h] = m_news[h]

    update(lambda h: ck_ref[0, h].astype(BF16), lambda h: cv_ref[0, h].astype(BF16),
           lambda h: cc_ref[0, h:h + 1, :], None)

    @pl.when(ki == pl.num_programs(1) - 1)
    def _():
        causal = (lax.broadcasted_iota(jnp.int32, (T, T), 1) <= lax.broadcasted_iota(jnp.int32, (T, T), 0))
        update(lambda h: kn_ref[:, sls[h]], lambda h: vn_ref[:, sls[h]], lambda h: cn_ref[0, h:h + 1, :], causal)
        for h, sl in enumerate(sls):
            o = acc_scr[:, sl] / l_scr[h] * _sigmoid(gate_ref[:, sl].astype(F32))
            o_ref[:, sl] = o.astype(o_ref.dtype)


def fox_sample(proj, cache_k, cache_v, k_new, v_new, c_col, c_cache, c_new, base, *, row0, nseq, T, H, tk):
    P = cache_k.shape[2]
    W = H * HEAD_DIM
    rb0 = row0 // T
    cache_spec = pl.BlockSpec((1, H, tk, HEAD_DIM), lambda b, i: (b, 0, i, 0))
    base, base_spec, aliases = _row_range_output(base, proj.shape[0], W, 9)
    new_spec = pl.BlockSpec((T, W), lambda b, i: (b, 0))
    return pl.pallas_call(
        functools.partial(_fox_sample_kernel, H=H, T=T),
        out_shape=jax.ShapeDtypeStruct((proj.shape[0], W), BF16),
        grid=(nseq, P // tk),
        in_specs=[pl.BlockSpec((T, W), lambda b, i: (rb0 + b, 0)),
                  pl.BlockSpec((T, W), lambda b, i: (rb0 + b, 1)),
                  cache_spec, cache_spec, new_spec, new_spec,
                  pl.BlockSpec((T, H), lambda b, i: (b, 0)),
                  pl.BlockSpec((1, H, tk), lambda b, i: (b, 0, i)),
                  pl.BlockSpec((1, H, T), lambda b, i: (b, 0, 0)), base_spec],
        out_specs=pl.BlockSpec((T, W), lambda b, i: (rb0 + b, 0)),
        scratch_shapes=[pltpu.VMEM((H, T, 1), F32), pltpu.VMEM((H, T, 1), F32), pltpu.VMEM((T, W), F32)],
        input_output_aliases=aliases,
        compiler_params=_params(("parallel", "arbitrary")),
        name="fox_sample",
    )(proj, proj, cache_k, cache_v, k_new, v_new, c_col, c_cache, c_new, base)


def _pad_cols(a, width):
    return jnp.pad(a, ((0, 0), (0, width - a.shape[1])))


def kernel(x_prompt, x_sample, state_gdn, state_conv, cache_k, cache_v, cache_logf, cache_mem_k, cache_mem_v, mem_prompt, norm_mix_pre, norm_mix_post, norm_mlp_pre, norm_mlp_post, w_in_a, conv_w_a, a_log, dt_bias, gdn_norm, w_in_b, norm_kv, w_kvf, b_f, norm_mem, w_mem_kv, w_o, w_up, w_down):
    Bp, Tp, D = x_prompt.shape
    Bs, Ts, _ = x_sample.shape
    n_a = w_in_a.shape[0]
    depth = w_o.shape[0]
    H = a_log.shape[1]
    tokd = H * HEAD_DIM
    qkvd = 3 * tokd
    P = cache_k.shape[1]
    nmem = mem_prompt.shape[1]
    mh = cache_mem_k.shape[3]
    memd = mh * HEAD_DIM
    Mp, Ms = Bp * Tp, Bs * Ts
    M = Mp + Ms
    TM = next(t for t in (1024, 512, 256, 128, 64, 32) if M % t == 0)
    TMH = max(TM // 2, 32)
    HR = -(-H // SUBLANES) * SUBLANES

    x = jnp.concatenate([x_prompt.reshape(Mp, D), x_sample.reshape(Ms, D)], axis=0)

    mem_rows = mem_prompt.reshape(Bp * nmem, D)
    wkv = w_mem_kv.astype(BF16)
    pkv = jnp.stack([norm_matmul(mem_rows, norm_mem[l], wkv, layer=l, tm=min(TM, Bp * nmem), tn=memd)
                     for l in range(depth)])
    p_mem_k = pkv[:, :, :memd].reshape(depth, Bp, nmem, mh, HEAD_DIM)
    p_mem_v = pkv[:, :, memd:].reshape(depth, Bp, nmem, mh, HEAD_DIM)
    wo_all = w_o.astype(BF16)
    wup_all = w_up.astype(BF16)
    wdn_all = w_down.astype(BF16)
    wb_all = w_in_b.astype(BF16)

    def hist8(h):
        return jnp.pad(h, ((0, 0), (SUBLANES - (CONV_W - 1), 0), (0, 0)))

    cache_k2 = cache_k.transpose(0, 2, 1, 3)
    cache_v2 = cache_v.transpose(0, 2, 1, 3)

    p_gdn, s_gdn, p_conv, s_conv = [], [], [], []
    lane = jnp.arange(LANES)
    for l in range(depth):
        if l == n_a:
            wk = w_kvf[:, :tokd].astype(BF16)
            wv = w_kvf[:, tokd:2 * tokd].astype(BF16)
            wf = _pad_cols(w_kvf[:, 2 * tokd:], LANES).astype(BF16)
            kp2, p_k = kv_proj(x, norm_kv, wk, row0=0, nseq=Bp, T=Tp, tm=TMH, H=H)
            vp2, p_v = kv_proj(x, norm_kv, wv, row0=0, nseq=Bp, T=Tp, tm=TMH, H=H)
            ks2, s_k = kv_proj(x, norm_kv, wk, row0=Mp, nseq=Bs, T=Ts, tm=TMH, H=H)
            vs2, s_v = kv_proj(x, norm_kv, wv, row0=Mp, nseq=Bs, T=Ts, tm=TMH, H=H)
            pf = jnp.zeros((SUBLANES, LANES), F32).at[0, :H].set(b_f)
            logf = norm_matmul(x, norm_kv, wf[None], tm=TM, tn=LANES, epilogue=_ep_log_forget, p=pf)[:, :H]
            lf_p = logf[:Mp].reshape(Bp, Tp, H)
            lf_s = logf[Mp:].reshape(Bs, Ts, H)

            def rows_of(a):
                a = jnp.pad(a.transpose(0, 2, 1), ((0, 0), (0, HR - H), (0, 0)))
                return a.reshape(a.shape[0] * HR, a.shape[2])

            cp_rows = cumsum_lanes(rows_of(lf_p), tb=min(Tp, 512)).reshape(Bp, HR, Tp)[:, :H]
            cp_col = cp_rows.transpose(0, 2, 1).reshape(Mp, H)
            tot = P + Ts
            tot_pad = -(-tot // LANES) * LANES
            lf_all = jnp.pad(jnp.concatenate([cache_logf, lf_s], axis=1), ((0, 0), (0, tot_pad - tot), (0, 0)))
            cs_rows = cumsum_lanes(rows_of(lf_all), tb=LANES).reshape(Bs, HR, tot_pad)[:, :H]
            cs_cache = cs_rows[:, :, :P]
            cs_new = cs_rows[:, :, P:tot]
            cs_col = cs_new.transpose(0, 2, 1).reshape(Ms, H)

        if l < n_a:
            w = w_in_a[l]
            o1 = qkvd + tokd
            w_main = jnp.concatenate([w[:, :o1], w[:, o1 + 2 * H:]], axis=1).astype(BF16)[None]
            w_ab = _pad_cols(w[:, o1:o1 + 2 * H], LANES).astype(BF16)[None]
            proj = norm_matmul(x, norm_mix_pre[l], w_main, tm=TM, tn=_col_tile(w_main.shape[2]))
            pg = jnp.zeros((SUBLANES, LANES), F32)
            pg = pg.at[0, :H].set(a_log[l]).at[1, :H].set(dt_bias[l]).at[2].set((lane < H).astype(F32))
            gb = norm_matmul(x, norm_mix_pre[l], w_ab, tm=TM, tn=LANES, epilogue=_ep_gdn_gates, p=pg)
            cw8 = jnp.pad(conv_w_a[l], ((0, SUBLANES - CONV_W), (0, 0)))
            tok, sp, cp = gdn_mix(proj, gb, hist8(jnp.zeros((Bp, CONV_W - 1, qkvd), F32)), cw8,
                                  jnp.zeros((Bp, H, HEAD_DIM, HEAD_DIM), F32), gdn_norm[l], None,
                                  row0=0, nseq=Bp, T=Tp, L=min(Tp, GDN_CHUNK), H=H)
            tok, ss, cs = gdn_mix(proj, gb, hist8(state_conv[l]), cw8, state_gdn[l], gdn_norm[l], tok,
                                  row0=Mp, nseq=Bs, T=Ts, L=min(Ts, GDN_CHUNK), H=H)
            p_gdn.append(sp)
            s_gdn.append(ss)
            p_conv.append(cp[:, SUBLANES - (CONV_W - 1):])
            s_conv.append(cs[:, SUBLANES - (CONV_W - 1):])
            mq_block = (qkvd + tokd) // memd
        else:
            proj = norm_matmul(x, norm_mix_pre[l], wb_all, layer=l - n_a, tm=TM, tn=_col_tile(wb_all.shape[2]),
                               out_dtype=BF16)
            tfox = min(Tp, 1024)
            tok = fox_prompt(proj, kp2, vp2, cp_col, cp_rows, nseq=Bp, T=Tp, H=H, t=tfox,
                             hb=2, rsub=max(tfox // 512, 1))
            tok = fox_sample(proj, cache_k2, cache_v2, ks2, vs2, cs_col, cs_cache, cs_new, tok,
                             row0=Mp, nseq=Bs, T=Ts, H=H, tk=min(P, 1024))
            mq_block = 2 * tokd // memd
        mem_o = mem_attend(proj, p_mem_k, p_mem_v, l, None, row0=0, rows_per_seq=Tp, tm=min(Tp, 512), sb=1,
                           col_block=mq_block)
        mem_o = mem_attend(proj, cache_mem_k, cache_mem_v, l, mem_o, row0=Mp, rows_per_seq=Ts, tm=Ts,
                           sb=math.gcd(Bs, 8), col_block=mq_block)
        x = out_proj(tok, mem_o, x, wo_all, l, norm_mix_post[l], tm=TMH)
        x = mlp(x, norm_mlp_pre[l], norm_mlp_post[l], wup_all, wdn_all, l, tm=TMH, tf=min(wup_all.shape[2], 1024))

    y_prompt = x[:Mp].reshape(Bp, Tp, D)
    y_sample = x[Mp:].reshape(Bs, Ts, D)
    return (y_prompt, y_sample, jnp.stack(p_gdn), jnp.stack(p_conv),
            p_k.transpose(0, 2, 1, 3), p_v.transpose(0, 2, 1, 3), lf_p, p_mem_k, p_mem_v,
            jnp.stack(s_gdn), jnp.stack(s_conv),
            s_k.transpose(0, 2, 1, 3), s_v.transpose(0, 2, 1, 3), lf_s)
```

```python
import functools
import math

import jax
import jax.numpy as jnp
from jax import lax
from jax.experimental import pallas as pl
from jax.experimental.pallas import tpu as pltpu

F32 = jnp.float32
BF16 = jnp.bfloat16
EPS = 1e-6
HEAD_DIM = 128
LANES = 128
SUBLANES = 8
SCALE = HEAD_DIM ** -0.5
LOG2E = math.log2(math.e)
CONV_W = 4
GDN_CHUNK = 64
GDN_STACK = 2
SOLVE_BLOCK = 16
HI = lax.Precision.HIGHEST
VMEM_LIMIT = 56 * 1024 * 1024


def _params(sem, vmem=VMEM_LIMIT):
    return pltpu.CompilerParams(dimension_semantics=sem, vmem_limit_bytes=vmem)


def _rms(x):
    return x * lax.rsqrt(jnp.mean(x * x, axis=-1, keepdims=True) + EPS)


def _sigmoid(x):
    return 1.0 / (1.0 + jnp.exp(-x))


def _softplus(x):
    return jnp.maximum(x, 0.0) + jnp.log1p(jnp.exp(-jnp.abs(x)))


def _dot(a, b, precision=None):
    return jnp.dot(a, b, preferred_element_type=F32, precision=precision)


def _dot_tn(a, b):
    return lax.dot_general(a, b, (((0,), (0,)), ((), ())), preferred_element_type=F32)


def _dot_nt(a, b):
    return lax.dot_general(a, b, (((1,), (1,)), ((), ())), preferred_element_type=F32)


def _split2(x):
    hi = x.astype(BF16)
    return hi, (x - hi.astype(F32)).astype(BF16)


def _split3(x):
    hi = x.astype(BF16)
    r = x - hi.astype(F32)
    mid = r.astype(BF16)
    return hi, mid, (r - mid.astype(F32)).astype(BF16)


def _dot3(a, b):
    return _dot(a[0], b[0]) + (_dot(a[0], b[1]) + _dot(a[1], b[0]))


def _ep_none(acc, p):
    return acc


def _ep_gdn_gates(acc, p):
    g = -jnp.exp(p[0:1]) * _softplus(acc + p[1:2])
    return jnp.where(p[2:3] > 0.5, g, _sigmoid(acc))


def _ep_log_forget(acc, p):
    return -_softplus(-(acc + p[0:1]))


def _norm_matmul_kernel(x_ref, g_ref, w_ref, p_ref, o_ref, h_ref, *, epilogue):
    @pl.when(pl.program_id(1) == 0)
    def _():
        h_ref[...] = (_rms(x_ref[...]) * g_ref[...]).astype(BF16)

    acc = _dot(h_ref[...], w_ref[...])
    o_ref[...] = epilogue(acc, p_ref[...]).astype(o_ref.dtype)


def _col_tile(n, cap=1792):
    return max(c for c in range(LANES, min(n, cap) + 1, LANES) if n % c == 0)


def norm_matmul(x, g, w, *, tm, tn, layer=0, out_dtype=F32, epilogue=_ep_none, p=None):
    M, K = x.shape
    N = w.shape[2]
    if p is None:
        p = jnp.zeros((SUBLANES, tn), F32)
    return pl.pallas_call(
        functools.partial(_norm_matmul_kernel, epilogue=epilogue),
        out_shape=jax.ShapeDtypeStruct((M, N), out_dtype),
        grid=(M // tm, N // tn),
        in_specs=[pl.BlockSpec((tm, K), lambda i, j: (i, 0)),
                  pl.BlockSpec((1, K), lambda i, j: (0, 0)),
                  pl.BlockSpec((None, K, tn), lambda i, j: (layer, 0, j)),
                  pl.BlockSpec((SUBLANES, tn), lambda i, j: (0, 0))],
        out_specs=pl.BlockSpec((tm, tn), lambda i, j: (i, j)),
        scratch_shapes=[pltpu.VMEM((tm, K), BF16)],
        compiler_params=_params(("parallel", "arbitrary")),
        name="norm_matmul",
    )(x, g.reshape(1, K), w, p)


def _kv_proj_kernel(x_ref, g_ref, w_ref, o2_ref, o4_ref, *, H, ns, tt):
    h = (_rms(x_ref[...]) * g_ref[...]).astype(BF16)
    acc = _dot(h, w_ref[...])
    o2_ref[...] = acc.astype(o2_ref.dtype)
    for s in range(ns):
        for hh in range(H):
            o4_ref[s, hh] = acc[s * tt:(s + 1) * tt, hh * HEAD_DIM:(hh + 1) * HEAD_DIM]


def kv_proj(x, g, w, *, row0, nseq, T, tm, H):
    K = x.shape[1]
    N = w.shape[1]
    rows = nseq * T
    rb0 = row0 // tm
    ns, tt = (tm // T, T) if tm >= T else (1, tm)
    parts = T // tt
    return pl.pallas_call(
        functools.partial(_kv_proj_kernel, H=H, ns=ns, tt=tt),
        out_shape=(jax.ShapeDtypeStruct((rows, N), BF16),
                   jax.ShapeDtypeStruct((nseq, H, T, HEAD_DIM), F32)),
        grid=(rows // tm,),
        in_specs=[pl.BlockSpec((tm, K), lambda i: (rb0 + i, 0)),
                  pl.BlockSpec((1, K), lambda i: (0, 0)),
                  pl.BlockSpec((K, N), lambda i: (0, 0))],
        out_specs=(pl.BlockSpec((tm, N), lambda i: (i, 0)),
                   pl.BlockSpec((ns, H, tt, HEAD_DIM), lambda i: (i // parts, 0, i % parts, 0))),
        compiler_params=_params(("parallel",)),
        name="kv_proj",
    )(x, g.reshape(1, K), w)


def _out_proj_kernel(tok_ref, mem_ref, x_ref, wa_ref, wb_ref, g_ref, o_ref):
    mix = _dot(tok_ref[...], wa_ref[...]) + _dot(mem_ref[...], wb_ref[...])
    o_ref[...] = x_ref[...] + _rms(mix) * g_ref[...]


def out_proj(tok, mem, x, w, layer, g, *, tm):
    M, D = x.shape
    Ka, Kb = tok.shape[1], mem.shape[1]
    assert Ka % Kb == 0
    return pl.pallas_call(
        _out_proj_kernel,
        out_shape=jax.ShapeDtypeStruct((M, D), F32),
        grid=(M // tm,),
        in_specs=[pl.BlockSpec((tm, Ka), lambda i: (i, 0)),
                  pl.BlockSpec((tm, Kb), lambda i: (i, 0)),
                  pl.BlockSpec((tm, D), lambda i: (i, 0)),
                  pl.BlockSpec((None, Ka, D), lambda i: (layer, 0, 0)),
                  pl.BlockSpec((None, Kb, D), lambda i: (layer, Ka // Kb, 0)),
                  pl.BlockSpec((1, D), lambda i: (0, 0))],
        out_specs=pl.BlockSpec((tm, D), lambda i: (i, 0)),
        compiler_params=_params(("parallel",)),
        name="out_proj",
    )(tok, mem, x, w, w, g.reshape(1, D))


def _mlp_kernel(x_ref, gpre_ref, gpost_ref, wup_ref, wdn_ref, o_ref, h_ref, acc_ref):
    j = pl.program_id(1)

    @pl.when(j == 0)
    def _():
        h_ref[...] = (_rms(x_ref[...]) * gpre_ref[...]).astype(BF16)
        acc_ref[...] = jnp.zeros_like(acc_ref)

    up = _dot(h_ref[...], wup_ref[...])
    act = jnp.square(jnp.maximum(up, 0.0)).astype(BF16)
    acc_ref[...] += _dot(act, wdn_ref[...])

    @pl.when(j == pl.num_programs(1) - 1)
    def _():
        o_ref[...] = x_ref[...] + _rms(acc_ref[...]) * gpost_ref[...]


def mlp(x, gpre, gpost, wup, wdn, layer, *, tm, tf):
    M, D = x.shape
    FF = wup.shape[2]
    return pl.pallas_call(
        _mlp_kernel,
        out_shape=jax.ShapeDtypeStruct((M, D), F32),
        grid=(M // tm, FF // tf),
        in_specs=[pl.BlockSpec((tm, D), lambda i, j: (i, 0)),
                  pl.BlockSpec((1, D), lambda i, j: (0, 0)),
                  pl.BlockSpec((1, D), lambda i, j: (0, 0)),
                  pl.BlockSpec((None, D, tf), lambda i, j: (layer, 0, j)),
                  pl.BlockSpec((None, tf, D), lambda i, j: (layer, j, 0))],
        out_specs=pl.BlockSpec((tm, D), lambda i, j: (i, 0)),
        scratch_shapes=[pltpu.VMEM((tm, D), BF16), pltpu.VMEM((tm, D), F32)],
        compiler_params=_params(("parallel", "arbitrary")),
        name="mlp",
    )(x, gpre.reshape(1, D), gpost.reshape(1, D), wup, wdn)


def _row_range_output(base, total_rows, width, n_inputs):
    spec = pl.BlockSpec(memory_space=pl.ANY)
    if base is None:
        return jnp.zeros((SUBLANES, LANES), BF16), spec, {}
    assert base.shape == (total_rows, width) and base.dtype == BF16
    return base, spec, {n_inputs: 0}


def _mem_attn_kernel(q_ref, mk_ref, mv_ref, base_ref, o_ref, *, heads, sb, tm):
    del base_ref
    probs = [(s, slice(s * tm, (s + 1) * tm), h, slice(h * HEAD_DIM, (h + 1) * HEAD_DIM))
             for s in range(sb) for h in range(heads)]
    ss = [_dot_nt(q_ref[rs, sl].astype(BF16), mk_ref[0, s, :, h, :].astype(BF16)) * SCALE
          for s, rs, h, sl in probs]
    es = [jnp.exp(s - jnp.max(s, axis=-1, keepdims=True)) for s in ss]
    ps = [(e / jnp.sum(e, axis=-1, keepdims=True)).astype(BF16) for e in es]
    outs = [_dot(p, mv_ref[0, s, :, h, :].astype(BF16)) for p, (s, _, h, _) in zip(ps, probs)]
    for o, (_, rs, _, sl) in zip(outs, probs):
        o_ref[rs, sl] = o.astype(o_ref.dtype)


def mem_attend(proj, mk, mv, layer, base, *, row0, rows_per_seq, tm, sb, col_block):
    _, nseq, nmem, heads, _ = mk.shape
    assert sb == 1 or tm == rows_per_seq
    W = heads * HEAD_DIM
    nt = rows_per_seq // tm
    rb0 = row0 // (sb * tm)
    mem_spec = pl.BlockSpec((1, sb, nmem, heads, HEAD_DIM), lambda b, i: (layer, b, 0, 0, 0))
    base, base_spec, aliases = _row_range_output(base, proj.shape[0], W, 3)
    return pl.pallas_call(
        functools.partial(_mem_attn_kernel, heads=heads, sb=sb, tm=tm),
        out_shape=jax.ShapeDtypeStruct((proj.shape[0], W), BF16),
        grid=(nseq // sb, nt),
        in_specs=[pl.BlockSpec((sb * tm, W), lambda b, i: (rb0 + b * nt + i, col_block)), mem_spec, mem_spec,
                  base_spec],
        out_specs=pl.BlockSpec((sb * tm, W), lambda b, i: (rb0 + b * nt + i, 0)),
        input_output_aliases=aliases,
        compiler_params=_params(("parallel", "parallel")),
        name="mem_attend",
    )(proj, mk, mv, base)


def _bdot(a, b):
    return _dot(a.astype(BF16), b.astype(BF16))


def _unit_lower_solve(As, rhss, n, L):
    row = lax.broadcasted_iota(jnp.int32, (n, n), 0)
    col = lax.broadcasted_iota(jnp.int32, (n, n), 1)
    shift = SOLVE_BLOCK.bit_length() - 1
    same = (row >> shift) == (col >> shift)
    es = [jnp.where(same, -A, 0.0) for A in As]
    offs = [jnp.where(same, 0.0, A) for A in As]
    ps = es
    span = 2
    while span < SOLVE_BLOCK:
        ps = [_bdot(p, p) for p in ps]
        es = [e + p + _bdot(e, p) for e, p in zip(es, ps)]
        span *= 2
    ys = [r + _bdot(e, r) for e, r in zip(es, rhss)]
    nblocks = L // SOLVE_BLOCK
    if nblocks > 1:
        powers = [[o + _bdot(e, o) for e, o in zip(es, offs)]]
        span = 2
        while span < nblocks:
            powers.append([_bdot(p, p) for p in powers[-1]])
            span *= 2
        for pw in reversed(powers[1:]):
            ys = [y + _bdot(p, y) for p, y in zip(pw, ys)]
        ys = [y - _bdot(p, y) for p, y in zip(powers[0], ys)]
    return ys


def _stack_heads(t, heads):
    return jnp.concatenate([t[:, h * HEAD_DIM:(h + 1) * HEAD_DIM] for h in heads], axis=0)


def _conv_silu(x, prev, w):
    row8 = lax.broadcasted_iota(jnp.int32, (SUBLANES, 1), 0)
    y = None
    for s in range(CONV_W - 1, 0, -1):
        sh = pltpu.roll(x, s, 0)
        top = jnp.where(row8 >= s, sh[:SUBLANES], pltpu.roll(prev, s, 0))
        term = jnp.concatenate([top, sh[SUBLANES:]], axis=0) * w[CONV_W - 1 - s:CONV_W - s]
        y = term if y is None else y + term
    y = y + x * w[CONV_W - 1:CONV_W]
    return y * _sigmoid(y)


def _gdn_mid(ys, gb, *, L, Hs, H):
    n = Hs * L
    lshift = L.bit_length() - 1
    yq, yk, yv = ys

    r2 = lax.broadcasted_iota(jnp.int32, (n, n), 0)
    c2 = lax.broadcasted_iota(jnp.int32, (n, n), 1)
    same = (r2 >> lshift) == (c2 >> lshift)
    incl = same & (r2 >= c2)
    strict = same & (r2 > c2)
    stack = _stack_heads
    groups = [[gi * Hs + j for j in range(Hs)] for gi in range(H // Hs)]

    rl = lax.broadcasted_iota(jnp.int32, (L, L), 0)
    cl = lax.broadcasted_iota(jnp.int32, (L, L), 1)
    tril = jnp.where(rl >= cl, 1.0, 0.0).astype(BF16)
    gcum = sum(_dot(tril, part) for part in _split3(gb))
    gcum_t = jnp.concatenate([gcum, jnp.zeros((LANES - L, LANES), F32)], axis=0).T if L < LANES else gcum.T

    def col(tile, heads, off=0):
        return jnp.concatenate([tile[:, off + h:off + h + 1] for h in heads], axis=0)

    Gs = [col(gcum, hs) for hs in groups]
    Grs = [jnp.concatenate([gcum_t[h:h + 1, :L] for h in hs], axis=1) for hs in groups]
    betas = [col(gb, hs, H) for hs in groups]
    glasts = [[gcum[L - 1:L, h:h + 1] for h in hs] for hs in groups]
    GLs = [jnp.concatenate([jnp.broadcast_to(g, (L, 1)) for g in gl], axis=0) for gl in glasts]

    def l2n(t):
        return t * lax.rsqrt(jnp.sum(t * t, axis=-1, keepdims=True) + EPS)

    qs = [l2n(stack(yq, hs)) * SCALE for hs in groups]
    ks = [l2n(stack(yk, hs)) for hs in groups]
    vs = [stack(yv, hs) for hs in groups]
    eGs = [jnp.exp(G) for G in Gs]
    decays = [jnp.exp(jnp.where(incl, G - Gr, -jnp.inf)) for G, Gr in zip(Gs, Grs)]
    kbs = [k.astype(BF16) for k in ks]
    kks = [_dot_nt(kb, kb) for kb in kbs]
    qks = [_dot_nt(q.astype(BF16), kb) for q, kb in zip(qs, kbs)]
    As = [jnp.where(strict, b * kk * d, 0.0) for b, kk, d in zip(betas, kks, decays)]
    rhss = [jnp.concatenate([b * v, (b * eG) * k], axis=-1) for b, v, eG, k in zip(betas, vs, eGs, ks)]
    sols = _unit_lower_solve(As, rhss, n, L)
    us = [s[:, :HEAD_DIM] for s in sols]
    ws = [s[:, HEAD_DIM:].astype(BF16) for s in sols]
    qes = [(q * eG).astype(BF16) for q, eG in zip(qs, eGs)]
    kds = [(k * jnp.exp(GL - G)).astype(BF16) for k, GL, G in zip(ks, GLs, Gs)]
    attns = [(qk * d).astype(BF16) for qk, d in zip(qks, decays)]
    return us, ws, qes, kds, attns, glasts


def _gdn_tail(front, z, gnorm, s_scr, o_ref, *, L, Hs, H):
    us, ws, qes, kds, attns, glasts = front
    groups = [[gi * Hs + j for j in range(Hs)] for gi in range(H // Hs)]
    rsl = [slice(j * L, (j + 1) * L) for j in range(Hs)]
    sbs = [[s_scr[h].astype(BF16) for h in hs] for hs in groups]
    v_news = [jnp.concatenate([u[rs] - _dot(w[rs], sb) for rs, sb in zip(rsl, sbg)], axis=0).astype(BF16)
              for u, w, sbg in zip(us, ws, sbs)]
    outs = [jnp.concatenate([_dot(qe[rs], sb) for rs, sb in zip(rsl, sbg)], axis=0) + _dot(at, vn)
            for qe, sbg, at, vn in zip(qes, sbs, attns, v_news)]
    for hs, gl, kd, vn in zip(groups, glasts, kds, v_news):
        for j, h in enumerate(hs):
            s_scr[h] = jnp.exp(gl[j]) * s_scr[h] + _dot_tn(kd[rsl[j]], vn[rsl[j]])
    for hs, out in zip(groups, outs):
        zz = _stack_heads(z, hs)
        o = (_rms(out) * gnorm * (zz * _sigmoid(zz))).astype(o_ref.dtype)
        for j, h in enumerate(hs):
            o_ref[:, h * HEAD_DIM:(h + 1) * HEAD_DIM] = o[rsl[j]]


def _gdn_kernel(q_ref, k_ref, v_ref, z_ref, gb_ref, hist_ref, cw_ref, s0_ref, gn_ref, base_ref,
                o_ref, sfin_ref, conv_ref,
                s_scr, prev_scr, y_scr, u_scr, w_scr, qe_scr, kd_scr, at_scr, gl_scr, *, L, Hs, H, lag):
    del base_ref
    s = pl.program_id(1)
    last = s == pl.num_programs(1) - 1
    tokd = H * HEAD_DIM
    kw = dict(L=L, Hs=Hs, H=H)
    ngroups = H // Hs

    def store(front):
        us, ws, qes, kds, attns, glasts = front
        for g in range(ngroups):
            u_scr[g], w_scr[g], qe_scr[g], kd_scr[g], at_scr[g] = us[g], ws[g], qes[g], kds[g], attns[g]
            for j, gl in enumerate(glasts[g]):
                gl_scr[g * Hs + j] = jnp.broadcast_to(gl, (SUBLANES, LANES))

    def load():
        rng = range(ngroups)
        glasts = [[gl_scr[g * Hs + j][0:1, 0:1] for j in range(Hs)] for g in rng]
        return ([u_scr[g] for g in rng], [w_scr[g] for g in rng], [qe_scr[g] for g in rng],
                [kd_scr[g] for g in rng], [at_scr[g] for g in rng], glasts)

    def prep():
        ys = []
        for t, ref in enumerate((q_ref, k_ref, v_ref)):
            cols = slice(t * tokd, (t + 1) * tokd)
            x = ref[...]
            ys.append(_conv_silu(x, prev_scr[:, cols], cw_ref[:, cols]))
            prev_scr[:, cols] = x[L - SUBLANES:]
        return ys

    @pl.when(s == 0)
    def _():
        prev_scr[...] = hist_ref[0]

    @pl.when(s == 2 * lag)
    def _():
        s_scr[...] = s0_ref[0]

    if lag:
        @pl.when(s == 0)
        def _():
            s_scr[...] = s0_ref[0]
            for scr in (y_scr, u_scr, w_scr, qe_scr, kd_scr, at_scr, gl_scr):
                scr[...] = jnp.zeros_like(scr)

        front = load()
        ys = [y_scr[t] for t in range(3)]
        _gdn_tail(front, z_ref[...], gn_ref[...], s_scr, o_ref, **kw)
        store(_gdn_mid(ys, gb_ref[...], **kw))
        for t, y in enumerate(prep()):
            y_scr[t] = y
    else:
        _gdn_tail(_gdn_mid(prep(), gb_ref[...], **kw), z_ref[...], gn_ref[...], s_scr, o_ref, **kw)

    @pl.when(last)
    def _():
        sfin_ref[0] = s_scr[...]
        conv_ref[0] = prev_scr[...]


def gdn_mix(proj, gb, hist8, cw8, s0, gnorm, base, *, row0, nseq, T, L, H):
    NC = T // L
    tokd = H * HEAD_DIM
    rb0 = row0 // L
    Hs = GDN_STACK
    n = Hs * L
    ng = H // Hs

    lag = 1 if NC > 1 else 0

    def col(group, behind):
        return lambda b, s: (rb0 + b * NC + jnp.clip(s - behind, 0, NC - 1), group)

    base, base_spec, aliases = _row_range_output(base, proj.shape[0], tokd, 9)
    return pl.pallas_call(
        functools.partial(_gdn_kernel, L=L, Hs=Hs, H=H, lag=lag),
        out_shape=(jax.ShapeDtypeStruct((proj.shape[0], tokd), BF16),
                   jax.ShapeDtypeStruct((nseq, H, HEAD_DIM, HEAD_DIM), F32),
                   jax.ShapeDtypeStruct((nseq, SUBLANES, 3 * tokd), F32)),
        grid=(nseq, NC + 2 * lag),
        in_specs=[pl.BlockSpec((L, tokd), col(0, 0)), pl.BlockSpec((L, tokd), col(1, 0)),
                  pl.BlockSpec((L, tokd), col(2, 0)), pl.BlockSpec((L, tokd), col(3, 2 * lag)),
                  pl.BlockSpec((L, LANES), col(0, lag)),
                  pl.BlockSpec((1, SUBLANES, 3 * tokd), lambda b, c: (b, 0, 0)),
                  pl.BlockSpec((SUBLANES, 3 * tokd), lambda b, c: (0, 0)),
                  pl.BlockSpec((1, H, HEAD_DIM, HEAD_DIM), lambda b, c: (b, 0, 0, 0)),
                  pl.BlockSpec((1, HEAD_DIM), lambda b, c: (0, 0)), base_spec],
        out_specs=(pl.BlockSpec((L, tokd), col(0, 2 * lag)),
                   pl.BlockSpec((1, H, HEAD_DIM, HEAD_DIM), lambda b, c: (b, 0, 0, 0)),
                   pl.BlockSpec((1, SUBLANES, 3 * tokd), lambda b, c: (b, 0, 0))),
        scratch_shapes=[pltpu.VMEM((H, HEAD_DIM, HEAD_DIM), F32),
                        pltpu.VMEM((SUBLANES, 3 * tokd), F32), pltpu.VMEM((3, L, tokd), F32),
                        pltpu.VMEM((ng, n, HEAD_DIM), F32), pltpu.VMEM((ng, n, HEAD_DIM), BF16),
                        pltpu.VMEM((ng, n, HEAD_DIM), BF16), pltpu.VMEM((ng, n, HEAD_DIM), BF16),
                        pltpu.VMEM((ng, n, n), BF16), pltpu.VMEM((H, SUBLANES, LANES), F32)],
        input_output_aliases=aliases,
        compiler_params=_params(("parallel", "arbitrary")),
        name="gdn_mix",
    )(proj, proj, proj, proj, gb, hist8, cw8, s0, gnorm.reshape(1, HEAD_DIM), base)


def _cumsum_kernel(x_ref, o_ref, carry_ref, *, tb):
    @pl.when(pl.program_id(0) == 0)
    def _():
        carry_ref[...] = jnp.zeros_like(carry_ref)

    r = lax.broadcasted_iota(jnp.int32, (tb, tb), 0)
    c = lax.broadcasted_iota(jnp.int32, (tb, tb), 1)
    triu = jnp.where(r <= c, 1.0, 0.0).astype(BF16)
    parts = _split3(x_ref[...])
    out = (_dot(parts[0], triu) + _dot(parts[1], triu) + _dot(parts[2], triu)) + carry_ref[...]
    o_ref[...] = out
    carry_ref[...] = out[:, tb - 1:tb]


def cumsum_lanes(x, *, tb):
    R, T = x.shape
    return pl.pallas_call(
        functools.partial(_cumsum_kernel, tb=tb),
        out_shape=jax.ShapeDtypeStruct((R, T), F32),
        grid=(T // tb,),
        in_specs=[pl.BlockSpec((R, tb), lambda i: (0, i))],
        out_specs=pl.BlockSpec((R, tb), lambda i: (0, i)),
        scratch_shapes=[pltpu.VMEM((R, 1), F32)],
        compiler_params=_params(("arbitrary",)),
        name="cumsum_lanes",
    )(x)


def _fox_prompt_kernel(qi_tab, ki_tab, q_ref, gate_ref, k_ref, v_ref, cq_ref, ck_ref, o_ref,
                       cq_scr, m_scr, acc_scr, *, t, hb, rsub):
    hg = pl.program_id(1)
    pair = pl.program_id(2)
    qi = qi_tab[pair]
    ki = ki_tab[pair]
    tr = t // rsub
    hsl = [slice(j * HEAD_DIM, (j + 1) * HEAD_DIM) for j in range(hb)]

    @pl.when(ki == 0)
    def _():
        lane = lax.broadcasted_iota(jnp.int32, cq_ref.shape, 1)
        for j in range(hb):
            col = jnp.sum(jnp.where(lane == hg * hb + j, cq_ref[...], 0.0), axis=1, keepdims=True) * LOG2E
            cq_scr[j] = jnp.broadcast_to(col, (t, LANES))
        m_scr[...] = jnp.full_like(m_scr, -jnp.inf)
        acc_scr[...] = jnp.zeros_like(acc_scr)

    def step(masked):
        probs = [(j, slice(r * tr, (r + 1) * tr), r * tr, (r + 1) * tr if masked else t)
                 for j in range(hb) for r in range(rsub)]
        ck2 = [ck_ref[0, j:j + 1, :] * LOG2E for j in range(hb)]
        v1 = [jnp.concatenate([v_ref[:, hsl[j]], jnp.ones((t, LANES), BF16)], axis=1) for j in range(hb)]
        ss = [_dot_nt(q_ref[rs, hsl[j]], k_ref[:kc, hsl[j]]) for j, rs, _, kc in probs]
        ss = [s * (SCALE * LOG2E) + (jnp.concatenate([cq_scr[j, rs]] * (kc // LANES), axis=1) - ck2[j][:, :kc])
              for s, (j, rs, _, kc) in zip(ss, probs)]
        if masked:
            ss = [jnp.where(lax.broadcasted_iota(jnp.int32, (tr, kc), 1)
                            <= lax.broadcasted_iota(jnp.int32, (tr, kc), 0) + r0, s, -jnp.inf)
                  for s, (_, _, r0, kc) in zip(ss, probs)]
        m_olds = [m_scr[j, rs] for j, rs, _, _ in probs]
        m_news = [jnp.maximum(mo, jnp.max(s, axis=-1, keepdims=True)) for mo, s in zip(m_olds, ss)]
        ps = [jnp.exp2(s - jnp.concatenate([mn] * (kc // LANES), axis=1)).astype(BF16)
              for s, mn, (_, _, _, kc) in zip(ss, m_news, probs)]
        pvs = [_dot(p, v1[j][:kc]) for p, (j, _, _, kc) in zip(ps, probs)]
        for (j, rs, _, _), mo, mn, pv in zip(probs, m_olds, m_news, pvs):
            alpha = jnp.exp2(mo - mn)
            acc_scr[j, rs] = jnp.concatenate([alpha, alpha], axis=1) * acc_scr[j, rs] + pv
            m_scr[j, rs] = mn

    @pl.when(ki < qi)
    def _():
        step(False)

    @pl.when(ki == qi)
    def _():
        step(True)
        for j in range(hb):
            acc = acc_scr[j]
            o = acc[:, :HEAD_DIM] / acc[:, HEAD_DIM:HEAD_DIM + 1]
            o_ref[:, hsl[j]] = (o * _sigmoid(gate_ref[:, hsl[j]].astype(F32))).astype(o_ref.dtype)


def fox_prompt(proj, karr, varr, c_col, c_rows, *, nseq, T, H, t, hb, rsub):
    nb = T // t
    ng = H // hb
    W = hb * HEAD_DIM
    pairs = [(qi, ki) for qi in range(nb) for ki in range(qi + 1)]
    qi_tab = jnp.asarray([p[0] for p in pairs], jnp.int32)
    ki_tab = jnp.asarray([p[1] for p in pairs], jnp.int32)
    grid_spec = pltpu.PrefetchScalarGridSpec(
        num_scalar_prefetch=2,
        grid=(nseq, ng, len(pairs)),
        in_specs=[pl.BlockSpec((t, W), lambda b, g, p, qt, kt: (b * nb + qt[p], g)),
                  pl.BlockSpec((t, W), lambda b, g, p, qt, kt: (b * nb + qt[p], ng + g)),
                  pl.BlockSpec((t, W), lambda b, g, p, qt, kt: (b * nb + kt[p], g)),
                  pl.BlockSpec((t, W), lambda b, g, p, qt, kt: (b * nb + kt[p], g)),
                  pl.BlockSpec((t, H), lambda b, g, p, qt, kt: (b * nb + qt[p], 0)),
                  pl.BlockSpec((1, hb, t), lambda b, g, p, qt, kt: (b * ng + g, 0, kt[p]))],
        out_specs=pl.BlockSpec((t, W), lambda b, g, p, qt, kt: (b * nb + qt[p], g)),
        scratch_shapes=[pltpu.VMEM((hb, t, LANES), F32), pltpu.VMEM((hb, t, LANES), F32),
                        pltpu.VMEM((hb, t, 2 * HEAD_DIM), F32)])
    return pl.pallas_call(
        functools.partial(_fox_prompt_kernel, t=t, hb=hb, rsub=rsub),
        out_shape=jax.ShapeDtypeStruct((proj.shape[0], H * HEAD_DIM), BF16),
        grid_spec=grid_spec,
        compiler_params=_params(("parallel", "parallel", "arbitrary")),
        name="fox_prompt",
    )(qi_tab, ki_tab, proj, proj, karr, varr, c_col, c_rows.reshape(nseq * ng, hb, T))


def _fox_sample_kernel(q_ref, gate_ref, ck_ref, cv_ref, kn_ref, vn_ref, cq_ref, cc_ref, cn_ref, base_ref, o_ref,
                       m_scr, l_scr, acc_scr, *, H, T):
    del base_ref
    ki = pl.program_id(1)

    @pl.when(ki == 0)
    def _():
        m_scr[...] = jnp.full_like(m_scr, -jnp.inf)
        l_scr[...] = jnp.zeros_like(l_scr)
        acc_scr[...] = jnp.zeros_like(acc_scr)

    cq = cq_ref[...]

    sls = [slice(h * HEAD_DIM, (h + 1) * HEAD_DIM) for h in range(H)]

    def update(k_of, v_of, ck_of, mask):
        ss = [_dot_nt(q_ref[:, sl], k_of(h)) * SCALE + (cq[:, h:h + 1] - ck_of(h)) for h, sl in enumerate(sls)]
        if mask is not None:
            ss = [jnp.where(mask, s, -jnp.inf) for s in ss]
        m_olds = [m_scr[h] for h in range(H)]
        m_news = [jnp.maximum(mo, jnp.max(s, axis=-1, keepdims=True)) for mo, s in zip(m_olds, ss)]
        alphas = [jnp.exp(mo - mn) for mo, mn in zip(m_olds, m_news)]
        ps = [jnp.exp(s - mn) for s, mn in zip(ss, m_news)]
        pvs = [_dot(p.astype(BF16), v_of(h)) for h, p in enumerate(ps)]
        for h, sl in enumerate(sls):
            l_scr[h] = alphas[h] * l_scr[h] + jnp.sum(ps[h], axis=-1, keepdims=True)
            acc_scr[:, sl] = alphas[h] * acc_scr[:, sl] + pvs[h]
            m_scr[h] = m_news[h]

    update(lambda h: ck_ref[0, h].astype(BF16), lambda h: cv_ref[0, h].astype(BF16),
           lambda h: cc_ref[0, h:h + 1, :], None)

    @pl.when(ki == pl.num_programs(1) - 1)
    def _():
        causal = (lax.broadcasted_iota(jnp.int32, (T, T), 1) <= lax.broadcasted_iota(jnp.int32, (T, T), 0))
        update(lambda h: kn_ref[:, sls[h]], lambda h: vn_ref[:, sls[h]], lambda h: cn_ref[0, h:h + 1, :], causal)
        for h, sl in enumerate(sls):
            o = acc_scr[:, sl] / l_scr[h] * _sigmoid(gate_ref[:, sl].astype(F32))
            o_ref[:, sl] = o.astype(o_ref.dtype)


def fox_sample(proj, cache_k, cache_v, k_new, v_new, c_col, c_cache, c_new, base, *, row0, nseq, T, H, tk):
    P = cache_k.shape[2]
    W = H * HEAD_DIM
    rb0 = row0 // T
    cache_spec = pl.BlockSpec((1, H, tk, HEAD_DIM), lambda b, i: (b, 0, i, 0))
    base, base_spec, aliases = _row_range_output(base, proj.shape[0], W, 9)
    new_spec = pl.BlockSpec((T, W), lambda b, i: (b, 0))
    return pl.pallas_call(
        functools.partial(_fox_sample_kernel, H=H, T=T),
        out_shape=jax.ShapeDtypeStruct((proj.shape[0], W), BF16),
        grid=(nseq, P // tk),
        in_specs=[pl.BlockSpec((T, W), lambda b, i: (rb0 + b, 0)),
                  pl.BlockSpec((T, W), lambda b, i: (rb0 + b, 1)),
                  cache_spec, cache_spec, new_spec, new_spec,
                  pl.BlockSpec((T, H), lambda b, i: (b, 0)),
                  pl.BlockSpec((1, H, tk), lambda b, i: (b, 0, i)),
                  pl.BlockSpec((1, H, T), lambda b, i: (b, 0, 0)), base_spec],
        out_specs=pl.BlockSpec((T, W), lambda b, i: (rb0 + b, 0)),
        scratch_shapes=[pltpu.VMEM((H, T, 1), F32), pltpu.VMEM((H, T, 1), F32), pltpu.VMEM((T, W), F32)],
        input_output_aliases=aliases,
        compiler_params=_params(("parallel", "arbitrary")),
        name="fox_sample",
    )(proj, proj, cache_k, cache_v, k_new, v_new, c_col, c_cache, c_new, base)


def _pad_cols(a, width):
    return jnp.pad(a, ((0, 0), (0, width - a.shape[1])))


def kernel(x_prompt, x_sample, state_gdn, state_conv, cache_k, cache_v, cache_logf, cache_mem_k, cache_mem_v, mem_prompt, norm_mix_pre, norm_mix_post, norm_mlp_pre, norm_mlp_post, w_in_a, conv_w_a, a_log, dt_bias, gdn_norm, w_in_b, norm_kv, w_kvf, b_f, norm_mem, w_mem_kv, w_o, w_up, w_down):
    Bp, Tp, D = x_prompt.shape
    Bs, Ts, _ = x_sample.shape
    n_a = w_in_a.shape[0]
    depth = w_o.shape[0]
    H = a_log.shape[1]
    tokd = H * HEAD_DIM
    qkvd = 3 * tokd
    P = cache_k.shape[1]
    nmem = mem_prompt.shape[1]
    mh = cache_mem_k.shape[3]
    memd = mh * HEAD_DIM
    Mp, Ms = Bp * Tp, Bs * Ts
    M = Mp + Ms
    TM = next(t for t in (1024, 512, 256, 128, 64, 32) if M % t == 0)
    TMH = max(TM // 2, 32)
    HR = -(-H // SUBLANES) * SUBLANES

    x = jnp.concatenate([x_prompt.reshape(Mp, D), x_sample.reshape(Ms, D)], axis=0)

    mem_rows = mem_prompt.reshape(Bp * nmem, D)
    wkv = w_mem_kv.astype(BF16)
    pkv = jnp.stack([norm_matmul(mem_rows, norm_mem[l], wkv, layer=l, tm=min(TM, Bp * nmem), tn=memd)
                     for l in range(depth)])
    p_mem_k = pkv[:, :, :memd].reshape(depth, Bp, nmem, mh, HEAD_DIM)
    p_mem_v = pkv[:, :, memd:].reshape(depth, Bp, nmem, mh, HEAD_DIM)
    wo_all = w_o.astype(BF16)
    wup_all = w_up.astype(BF16)
    wdn_all = w_down.astype(BF16)
    wb_all = w_in_b.astype(BF16)

    def hist8(h):
        return jnp.pad(h, ((0, 0), (SUBLANES - (CONV_W - 1), 0), (0, 0)))

    cache_k2 = cache_k.transpose(0, 2, 1, 3)
    cache_v2 = cache_v.transpose(0, 2, 1, 3)

    p_gdn, s_gdn, p_conv, s_conv = [], [], [], []
    lane = jnp.arange(LANES)
    for l in range(depth):
        if l == n_a:
            wk = w_kvf[:, :tokd].astype(BF16)
            wv = w_kvf[:, tokd:2 * tokd].astype(BF16)
            wf = _pad_cols(w_kvf[:, 2 * tokd:], LANES).astype(BF16)
            kp2, p_k = kv_proj(x, norm_kv, wk, row0=0, nseq=Bp, T=Tp, tm=TMH, H=H)
            vp2, p_v = kv_proj(x, norm_kv, wv, row0=0, nseq=Bp, T=Tp, tm=TMH, H=H)
            ks2, s_k = kv_proj(x, norm_kv, wk, row0=Mp, nseq=Bs, T=Ts, tm=TMH, H=H)
            vs2, s_v = kv_proj(x, norm_kv, wv, row0=Mp, nseq=Bs, T=Ts, tm=TMH, H=H)
            pf = jnp.zeros((SUBLANES, LANES), F32).at[0, :H].set(b_f)
            logf = norm_matmul(x, norm_kv, wf[None], tm=TM, tn=LANES, epilogue=_ep_log_forget, p=pf)[:, :H]
            lf_p = logf[:Mp].reshape(Bp, Tp, H)
            lf_s = logf[Mp:].reshape(Bs, Ts, H)

            def rows_of(a):
                a = jnp.pad(a.transpose(0, 2, 1), ((0, 0), (0, HR - H), (0, 0)))
                return a.reshape(a.shape[0] * HR, a.shape[2])

            cp_rows = cumsum_lanes(rows_of(lf_p), tb=min(Tp, 512)).reshape(Bp, HR, Tp)[:, :H]
            cp_col = cp_rows.transpose(0, 2, 1).reshape(Mp, H)
            tot = P + Ts
            tot_pad = -(-tot // LANES) * LANES
            lf_all = jnp.pad(jnp.concatenate([cache_logf, lf_s], axis=1), ((0, 0), (0, tot_pad - tot), (0, 0)))
            cs_rows = cumsum_lanes(rows_of(lf_all), tb=LANES).reshape(Bs, HR, tot_pad)[:, :H]
            cs_cache = cs_rows[:, :, :P]
            cs_new = cs_rows[:, :, P:tot]
            cs_col = cs_new.transpose(0, 2, 1).reshape(Ms, H)

        if l < n_a:
            w = w_in_a[l]
            o1 = qkvd + tokd
            w_main = jnp.concatenate([w[:, :o1], w[:, o1 + 2 * H:]], axis=1).astype(BF16)[None]
            w_ab = _pad_cols(w[:, o1:o1 + 2 * H], LANES).astype(BF16)[None]
            proj = norm_matmul(x, norm_mix_pre[l], w_main, tm=TM, tn=_col_tile(w_main.shape[2]))
            pg = jnp.zeros((SUBLANES, LANES), F32)
            pg = pg.at[0, :H].set(a_log[l]).at[1, :H].set(dt_bias[l]).at[2].set((lane < H).astype(F32))
            gb = norm_matmul(x, norm_mix_pre[l], w_ab, tm=TM, tn=LANES, epilogue=_ep_gdn_gates, p=pg)
            cw8 = jnp.pad(conv_w_a[l], ((0, SUBLANES - CONV_W), (0, 0)))
            tok, sp, cp = gdn_mix(proj, gb, hist8(jnp.zeros((Bp, CONV_W - 1, qkvd), F32)), cw8,
                                  jnp.zeros((Bp, H, HEAD_DIM, HEAD_DIM), F32), gdn_norm[l], None,
                                  row0=0, nseq=Bp, T=Tp, L=min(Tp, GDN_CHUNK), H=H)
            tok, ss, cs = gdn_mix(proj, gb, hist8(state_conv[l]), cw8, state_gdn[l], gdn_norm[l], tok,
                                  row0=Mp, nseq=Bs, T=Ts, L=min(Ts, GDN_CHUNK), H=H)
            p_gdn.append(sp)
            s_gdn.append(ss)
            p_conv.append(cp[:, SUBLANES - (CONV_W - 1):])
            s_conv.append(cs[:, SUBLANES - (CONV_W - 1):])
            mq_block = (qkvd + tokd) // memd
        else:
            proj = norm_matmul(x, norm_mix_pre[l], wb_all, layer=l - n_a, tm=TM, tn=_col_tile(wb_all.shape[2]),
                               out_dtype=BF16)
            tfox = min(Tp, 1024)
            tok = fox_prompt(proj, kp2, vp2, cp_col, cp_rows, nseq=Bp, T=Tp, H=H, t=tfox,
                             hb=2, rsub=max(tfox // 256, 1))
            tok = fox_sample(proj, cache_k2, cache_v2, ks2, vs2, cs_col, cs_cache, cs_new, tok,
                             row0=Mp, nseq=Bs, T=Ts, H=H, tk=min(P, 1024))
            mq_block = 2 * tokd // memd
        mem_o = mem_attend(proj, p_mem_k, p_mem_v, l, None, row0=0, rows_per_seq=Tp, tm=min(Tp, 512), sb=1,
                           col_block=mq_block)
        mem_o = mem_attend(proj, cache_mem_k, cache_mem_v, l, mem_o, row0=Mp, rows_per_seq=Ts, tm=Ts,
                           sb=math.gcd(Bs, 8), col_block=mq_block)
        x = out_proj(tok, mem_o, x, wo_all, l, norm_mix_post[l], tm=TMH)
        x = mlp(x, norm_mlp_pre[l], norm_mlp_post[l], wup_all, wdn_all, l, tm=TMH, tf=min(wup_all.shape[2], 1024))

    y_prompt = x[:Mp].reshape(Bp, Tp, D)
    y_sample = x[Mp:].reshape(Bs, Ts, D)
    return (y_prompt, y_sample, jnp.stack(p_gdn), jnp.stack(p_conv),
            p_k.transpose(0, 2, 1, 3), p_v.transpose(0, 2, 1, 3), lf_p, p_mem_k, p_mem_v,
            jnp.stack(s_gdn), jnp.stack(s_conv),
            s_k.transpose(0, 2, 1, 3), s_v.transpose(0, 2, 1, 3), lf_s)
```

```python
import functools
import math

import jax
import jax.numpy as jnp
from jax import lax
from jax.experimental import pallas as pl
from jax.experimental.pallas import tpu as pltpu

F32 = jnp.float32
BF16 = jnp.bfloat16
EPS = 1e-6
HEAD_DIM = 128
LANES = 128
SUBLANES = 8
SCALE = HEAD_DIM ** -0.5
LOG2E = math.log2(math.e)
CONV_W = 4
GDN_CHUNK = 64
GDN_STACK = 2
SOLVE_BLOCK = 16
V7X_VMEM_BYTES = 64 * 1024 * 1024
VMEM_LIMIT = V7X_VMEM_BYTES * 7 // 8
MAX_COL_TILE = 1792


def _params(sem):
    return pltpu.CompilerParams(dimension_semantics=sem, vmem_limit_bytes=VMEM_LIMIT)


def _rms(x):
    return x * lax.rsqrt(jnp.mean(x * x, axis=-1, keepdims=True) + EPS)


def _sigmoid(x):
    return 1.0 / (1.0 + jnp.exp(-x))


def _softplus(x):
    return jnp.maximum(x, 0.0) + jnp.log1p(jnp.exp(-jnp.abs(x)))


def _dot(a, b):
    return jnp.dot(a, b, preferred_element_type=F32)


def _dot_tn(a, b):
    return lax.dot_general(a, b, (((0,), (0,)), ((), ())), preferred_element_type=F32)


def _dot_nt(a, b):
    return lax.dot_general(a, b, (((1,), (1,)), ((), ())), preferred_element_type=F32)


def _split3(x):
    hi = x.astype(BF16)
    r = x - hi.astype(F32)
    mid = r.astype(BF16)
    return hi, mid, (r - mid.astype(F32)).astype(BF16)


def _ep_none(acc, p):
    return acc


def _ep_gdn_gates(acc, p):
    g = -jnp.exp(p[0:1]) * _softplus(acc + p[1:2])
    return jnp.where(p[2:3] > 0.5, g, _sigmoid(acc))


def _ep_log_forget(acc, p):
    return -_softplus(-(acc + p[0:1]))


def _norm_matmul_kernel(x_ref, g_ref, w_ref, p_ref, o_ref, h_ref, *, epilogue):
    @pl.when(pl.program_id(1) == 0)
    def _():
        h_ref[...] = (_rms(x_ref[...]) * g_ref[...]).astype(BF16)

    acc = _dot(h_ref[...], w_ref[...])
    o_ref[...] = epilogue(acc, p_ref[...]).astype(o_ref.dtype)


def _norm_matmul_side_kernel(x_ref, g_ref, w_ref, ws_ref, ps_ref, o_ref, os_ref, h_ref, *, side_epilogue):
    @pl.when(pl.program_id(1) == 0)
    def _():
        h = (_rms(x_ref[...]) * g_ref[...]).astype(BF16)
        h_ref[...] = h
        os_ref[...] = side_epilogue(_dot(h, ws_ref[...]), ps_ref[...])

    o_ref[...] = _dot(h_ref[...], w_ref[...]).astype(o_ref.dtype)


def norm_matmul_side(x, g, w, w_side, p_side, side_epilogue, *, tm, tn):
    M, K = x.shape
    N = w.shape[1]
    return pl.pallas_call(
        functools.partial(_norm_matmul_side_kernel, side_epilogue=side_epilogue),
        out_shape=(jax.ShapeDtypeStruct((M, N), F32), jax.ShapeDtypeStruct((M, LANES), F32)),
        grid=(M // tm, N // tn),
        in_specs=[pl.BlockSpec((tm, K), lambda i, j: (i, 0)),
                  pl.BlockSpec((1, K), lambda i, j: (0, 0)),
                  pl.BlockSpec((K, tn), lambda i, j: (0, j)),
                  pl.BlockSpec((K, LANES), lambda i, j: (0, 0)),
                  pl.BlockSpec((SUBLANES, LANES), lambda i, j: (0, 0))],
        out_specs=(pl.BlockSpec((tm, tn), lambda i, j: (i, j)),
                   pl.BlockSpec((tm, LANES), lambda i, j: (i, 0))),
        scratch_shapes=[pltpu.VMEM((tm, K), BF16)],
        compiler_params=_params(("parallel", "arbitrary")),
        name="norm_matmul_side",
    )(x, g.reshape(1, K), w, w_side, p_side)


def _col_tile(n):
    return max(c for c in range(LANES, min(n, MAX_COL_TILE) + 1, LANES) if n % c == 0)


def norm_matmul(x, g, w, *, tm, tn, layer=0, out_dtype=F32, epilogue=_ep_none, p=None):
    M, K = x.shape
    N = w.shape[2]
    if p is None:
        p = jnp.zeros((SUBLANES, tn), F32)
    return pl.pallas_call(
        functools.partial(_norm_matmul_kernel, epilogue=epilogue),
        out_shape=jax.ShapeDtypeStruct((M, N), out_dtype),
        grid=(M // tm, N // tn),
        in_specs=[pl.BlockSpec((tm, K), lambda i, j: (i, 0)),
                  pl.BlockSpec((1, K), lambda i, j: (0, 0)),
                  pl.BlockSpec((None, K, tn), lambda i, j: (layer, 0, j)),
                  pl.BlockSpec((SUBLANES, tn), lambda i, j: (0, 0))],
        out_specs=pl.BlockSpec((tm, tn), lambda i, j: (i, j)),
        scratch_shapes=[pltpu.VMEM((tm, K), BF16)],
        compiler_params=_params(("parallel", "arbitrary")),
        name="norm_matmul",
    )(x, g.reshape(1, K), w, p)


def _kv_proj_kernel(x_ref, g_ref, w_ref, o2_ref, o4_ref, *, H, ns, tt):
    h = (_rms(x_ref[...]) * g_ref[...]).astype(BF16)
    acc = _dot(h, w_ref[...])
    o2_ref[...] = acc.astype(o2_ref.dtype)
    for s in range(ns):
        for hh in range(H):
            o4_ref[s, hh] = acc[s * tt:(s + 1) * tt, hh * HEAD_DIM:(hh + 1) * HEAD_DIM]


def kv_proj(x, g, w, *, row0, nseq, T, tm, H):
    K = x.shape[1]
    N = w.shape[1]
    rows = nseq * T
    rb0 = row0 // tm
    ns, tt = (tm // T, T) if tm >= T else (1, tm)
    parts = T // tt
    return pl.pallas_call(
        functools.partial(_kv_proj_kernel, H=H, ns=ns, tt=tt),
        out_shape=(jax.ShapeDtypeStruct((rows, N), BF16),
                   jax.ShapeDtypeStruct((nseq, H, T, HEAD_DIM), F32)),
        grid=(rows // tm,),
        in_specs=[pl.BlockSpec((tm, K), lambda i: (rb0 + i, 0)),
                  pl.BlockSpec((1, K), lambda i: (0, 0)),
                  pl.BlockSpec((K, N), lambda i: (0, 0))],
        out_specs=(pl.BlockSpec((tm, N), lambda i: (i, 0)),
                   pl.BlockSpec((ns, H, tt, HEAD_DIM), lambda i: (i // parts, 0, i % parts, 0))),
        compiler_params=_params(("parallel",)),
        name="kv_proj",
    )(x, g.reshape(1, K), w)


def _out_proj_kernel(tok_ref, mem_ref, x_ref, wa_ref, wb_ref, g_ref, o_ref):
    mix = _dot(tok_ref[...], wa_ref[...]) + _dot(mem_ref[...], wb_ref[...])
    o_ref[...] = x_ref[...] + _rms(mix) * g_ref[...]


def out_proj(tok, mem, x, w, layer, g, *, tm):
    M, D = x.shape
    Ka, Kb = tok.shape[1], mem.shape[1]
    assert Ka % Kb == 0
    return pl.pallas_call(
        _out_proj_kernel,
        out_shape=jax.ShapeDtypeStruct((M, D), F32),
        grid=(M // tm,),
        in_specs=[pl.BlockSpec((tm, Ka), lambda i: (i, 0)),
                  pl.BlockSpec((tm, Kb), lambda i: (i, 0)),
                  pl.BlockSpec((tm, D), lambda i: (i, 0)),
                  pl.BlockSpec((None, Ka, D), lambda i: (layer, 0, 0)),
                  pl.BlockSpec((None, Kb, D), lambda i: (layer, Ka // Kb, 0)),
                  pl.BlockSpec((1, D), lambda i: (0, 0))],
        out_specs=pl.BlockSpec((tm, D), lambda i: (i, 0)),
        compiler_params=_params(("parallel",)),
        name="out_proj",
    )(tok, mem, x, w, w, g.reshape(1, D))


def _mlp_kernel(x_ref, gpre_ref, gpost_ref, wup_ref, wdn_ref, o_ref, h_ref, acc_ref):
    j = pl.program_id(1)

    @pl.when(j == 0)
    def _():
        h_ref[...] = (_rms(x_ref[...]) * gpre_ref[...]).astype(BF16)
        acc_ref[...] = jnp.zeros_like(acc_ref)

    up = _dot(h_ref[...], wup_ref[...])
    act = jnp.square(jnp.maximum(up, 0.0)).astype(BF16)
    acc_ref[...] += _dot(act, wdn_ref[...])

    @pl.when(j == pl.num_programs(1) - 1)
    def _():
        o_ref[...] = x_ref[...] + _rms(acc_ref[...]) * gpost_ref[...]


def mlp(x, gpre, gpost, wup, wdn, layer, *, tm, tf):
    M, D = x.shape
    FF = wup.shape[2]
    return pl.pallas_call(
        _mlp_kernel,
        out_shape=jax.ShapeDtypeStruct((M, D), F32),
        grid=(M // tm, FF // tf),
        in_specs=[pl.BlockSpec((tm, D), lambda i, j: (i, 0)),
                  pl.BlockSpec((1, D), lambda i, j: (0, 0)),
                  pl.BlockSpec((1, D), lambda i, j: (0, 0)),
                  pl.BlockSpec((None, D, tf), lambda i, j: (layer, 0, j)),
                  pl.BlockSpec((None, tf, D), lambda i, j: (layer, j, 0))],
        out_specs=pl.BlockSpec((tm, D), lambda i, j: (i, 0)),
        scratch_shapes=[pltpu.VMEM((tm, D), BF16), pltpu.VMEM((tm, D), F32)],
        compiler_params=_params(("parallel", "arbitrary")),
        name="mlp",
    )(x, gpre.reshape(1, D), gpost.reshape(1, D), wup, wdn)


def _row_range_output(base, total_rows, width, n_inputs):
    spec = pl.BlockSpec(memory_space=pl.ANY)
    if base is None:
        return jnp.zeros((SUBLANES, LANES), BF16), spec, {}
    assert base.shape == (total_rows, width) and base.dtype == BF16
    return base, spec, {n_inputs: 0}


def _mem_attn_kernel(q_ref, mk_ref, mv_ref, base_ref, o_ref, *, heads, sb, tm):
    del base_ref
    probs = [(s, slice(s * tm, (s + 1) * tm), h, slice(h * HEAD_DIM, (h + 1) * HEAD_DIM))
             for s in range(sb) for h in range(heads)]
    ss = [_dot_nt(q_ref[rs, sl].astype(BF16), mk_ref[0, s, :, h, :].astype(BF16)) * SCALE
          for s, rs, h, sl in probs]
    es = [jnp.exp(s - jnp.max(s, axis=-1, keepdims=True)) for s in ss]
    ps = [(e / jnp.sum(e, axis=-1, keepdims=True)).astype(BF16) for e in es]
    outs = [_dot(p, mv_ref[0, s, :, h, :].astype(BF16)) for p, (s, _, h, _) in zip(ps, probs)]
    for o, (_, rs, _, sl) in zip(outs, probs):
        o_ref[rs, sl] = o.astype(o_ref.dtype)


def mem_attend(proj, mk, mv, layer, base, *, row0, rows_per_seq, tm, sb, col_block):
    _, nseq, nmem, heads, _ = mk.shape
    assert sb == 1 or tm == rows_per_seq
    W = heads * HEAD_DIM
    nt = rows_per_seq // tm
    rb0 = row0 // (sb * tm)
    mem_spec = pl.BlockSpec((1, sb, nmem, heads, HEAD_DIM), lambda b, i: (layer, b, 0, 0, 0))
    base, base_spec, aliases = _row_range_output(base, proj.shape[0], W, 3)
    return pl.pallas_call(
        functools.partial(_mem_attn_kernel, heads=heads, sb=sb, tm=tm),
        out_shape=jax.ShapeDtypeStruct((proj.shape[0], W), BF16),
        grid=(nseq // sb, nt),
        in_specs=[pl.BlockSpec((sb * tm, W), lambda b, i: (rb0 + b * nt + i, col_block)), mem_spec, mem_spec,
                  base_spec],
        out_specs=pl.BlockSpec((sb * tm, W), lambda b, i: (rb0 + b * nt + i, 0)),
        input_output_aliases=aliases,
        compiler_params=_params(("parallel", "parallel")),
        name="mem_attend",
    )(proj, mk, mv, base)


def _bdot(a, b):
    return _dot(a.astype(BF16), b.astype(BF16))


def _unit_lower_solve(As, rhss, n, L):
    row = lax.broadcasted_iota(jnp.int32, (n, n), 0)
    col = lax.broadcasted_iota(jnp.int32, (n, n), 1)
    shift = SOLVE_BLOCK.bit_length() - 1
    same = (row >> shift) == (col >> shift)
    es = [jnp.where(same, -A, 0.0) for A in As]
    offs = [jnp.where(same, 0.0, A) for A in As]
    ps = es
    span = 2
    while span < SOLVE_BLOCK:
        ps = [_bdot(p, p) for p in ps]
        es = [e + p + _bdot(e, p) for e, p in zip(es, ps)]
        span *= 2
    ys = [r + _bdot(e, r) for e, r in zip(es, rhss)]
    nblocks = L // SOLVE_BLOCK
    if nblocks > 1:
        powers = [[o + _bdot(e, o) for e, o in zip(es, offs)]]
        span = 2
        while span < nblocks:
            powers.append([_bdot(p, p) for p in powers[-1]])
            span *= 2
        for pw in reversed(powers[1:]):
            ys = [y + _bdot(p, y) for p, y in zip(pw, ys)]
        ys = [y - _bdot(p, y) for p, y in zip(powers[0], ys)]
    return ys


def _stack_heads(t, heads):
    return jnp.concatenate([t[:, h * HEAD_DIM:(h + 1) * HEAD_DIM] for h in heads], axis=0)


def _conv_silu(x, prev, w):
    row8 = lax.broadcasted_iota(jnp.int32, (SUBLANES, 1), 0)
    y = None
    for s in range(CONV_W - 1, 0, -1):
        sh = pltpu.roll(x, s, 0)
        top = jnp.where(row8 >= s, sh[:SUBLANES], pltpu.roll(prev, s, 0))
        term = jnp.concatenate([top, sh[SUBLANES:]], axis=0) * w[CONV_W - 1 - s:CONV_W - s]
        y = term if y is None else y + term
    y = y + x * w[CONV_W - 1:CONV_W]
    return y * _sigmoid(y)


def _gdn_mid(ys, gb, *, L, Hs, H):
    n = Hs * L
    lshift = L.bit_length() - 1
    yq, yk, yv = ys

    r2 = lax.broadcasted_iota(jnp.int32, (n, n), 0)
    c2 = lax.broadcasted_iota(jnp.int32, (n, n), 1)
    same = (r2 >> lshift) == (c2 >> lshift)
    incl = same & (r2 >= c2)
    strict = same & (r2 > c2)
    stack = _stack_heads
    groups = [[gi * Hs + j for j in range(Hs)] for gi in range(H // Hs)]

    rl = lax.broadcasted_iota(jnp.int32, (L, L), 0)
    cl = lax.broadcasted_iota(jnp.int32, (L, L), 1)
    tril = jnp.where(rl >= cl, 1.0, 0.0).astype(BF16)
    gcum = sum(_dot(tril, part) for part in _split3(gb))
    gcum_t = jnp.concatenate([gcum, jnp.zeros((LANES - L, LANES), F32)], axis=0).T if L < LANES else gcum.T

    def col(tile, heads, off=0):
        return jnp.concatenate([tile[:, off + h:off + h + 1] for h in heads], axis=0)

    Gs = [col(gcum, hs) for hs in groups]
    Grs = [jnp.concatenate([gcum_t[h:h + 1, :L] for h in hs], axis=1) for hs in groups]
    betas = [col(gb, hs, H) for hs in groups]
    glasts = [[gcum[L - 1:L, h:h + 1] for h in hs] for hs in groups]
    GLs = [jnp.concatenate([jnp.broadcast_to(g, (L, 1)) for g in gl], axis=0) for gl in glasts]

    def l2n(t):
        return t * lax.rsqrt(jnp.sum(t * t, axis=-1, keepdims=True) + EPS)

    qs = [l2n(stack(yq, hs)) * SCALE for hs in groups]
    ks = [l2n(stack(yk, hs)) for hs in groups]
    vs = [stack(yv, hs) for hs in groups]
    eGs = [jnp.exp(G) for G in Gs]
    decays = [jnp.exp(jnp.where(incl, G - Gr, -jnp.inf)) for G, Gr in zip(Gs, Grs)]
    kbs = [k.astype(BF16) for k in ks]
    kks = [_dot_nt(kb, kb) for kb in kbs]
    qks = [_dot_nt(q.astype(BF16), kb) for q, kb in zip(qs, kbs)]
    As = [jnp.where(strict, b * kk * d, 0.0) for b, kk, d in zip(betas, kks, decays)]
    rhss = [jnp.concatenate([b * v, (b * eG) * k], axis=-1) for b, v, eG, k in zip(betas, vs, eGs, ks)]
    sols = _unit_lower_solve(As, rhss, n, L)
    us = [s[:, :HEAD_DIM] for s in sols]
    ws = [s[:, HEAD_DIM:].astype(BF16) for s in sols]
    qes = [(q * eG).astype(BF16) for q, eG in zip(qs, eGs)]
    kds = [(k * jnp.exp(GL - G)).astype(BF16) for k, GL, G in zip(ks, GLs, Gs)]
    attns = [(qk * d).astype(BF16) for qk, d in zip(qks, decays)]
    return us, ws, qes, kds, attns, glasts


def _gdn_tail(front, z, gnorm, s_scr, o_ref, *, L, Hs, H):
    us, ws, qes, kds, attns, glasts = front
    groups = [[gi * Hs + j for j in range(Hs)] for gi in range(H // Hs)]
    rsl = [slice(j * L, (j + 1) * L) for j in range(Hs)]
    sbs = [[s_scr[h].astype(BF16) for h in hs] for hs in groups]
    v_news = [jnp.concatenate([u[rs] - _dot(w[rs], sb) for rs, sb in zip(rsl, sbg)], axis=0).astype(BF16)
              for u, w, sbg in zip(us, ws, sbs)]
    outs = [jnp.concatenate([_dot(qe[rs], sb) for rs, sb in zip(rsl, sbg)], axis=0) + _dot(at, vn)
            for qe, sbg, at, vn in zip(qes, sbs, attns, v_news)]
    for hs, gl, kd, vn in zip(groups, glasts, kds, v_news):
        for j, h in enumerate(hs):
            s_scr[h] = jnp.exp(gl[j]) * s_scr[h] + _dot_tn(kd[rsl[j]], vn[rsl[j]])
    for hs, out in zip(groups, outs):
        zz = _stack_heads(z, hs)
        o = (_rms(out) * gnorm * (zz * _sigmoid(zz))).astype(o_ref.dtype)
        for j, h in enumerate(hs):
            o_ref[:, h * HEAD_DIM:(h + 1) * HEAD_DIM] = o[rsl[j]]


def _gdn_kernel(q_ref, k_ref, v_ref, z_ref, gb_ref, hist_ref, cw_ref, s0_ref, gn_ref, base_ref,
                o_ref, sfin_ref, conv_ref,
                s_scr, prev_scr, y_scr, u_scr, w_scr, qe_scr, kd_scr, at_scr, gl_scr, *, L, Hs, H, lag):
    del base_ref
    s = pl.program_id(1)
    last = s == pl.num_programs(1) - 1
    tokd = H * HEAD_DIM
    kw = dict(L=L, Hs=Hs, H=H)
    ngroups = H // Hs

    def store(front):
        us, ws, qes, kds, attns, glasts = front
        for g in range(ngroups):
            u_scr[g], w_scr[g], qe_scr[g], kd_scr[g], at_scr[g] = us[g], ws[g], qes[g], kds[g], attns[g]
            for j, gl in enumerate(glasts[g]):
                gl_scr[g * Hs + j] = jnp.broadcast_to(gl, (SUBLANES, LANES))

    def load():
        rng = range(ngroups)
        glasts = [[gl_scr[g * Hs + j][0:1, 0:1] for j in range(Hs)] for g in rng]
        return ([u_scr[g] for g in rng], [w_scr[g] for g in rng], [qe_scr[g] for g in rng],
                [kd_scr[g] for g in rng], [at_scr[g] for g in rng], glasts)

    def prep():
        ys = []
        for t, ref in enumerate((q_ref, k_ref, v_ref)):
            cols = slice(t * tokd, (t + 1) * tokd)
            x = ref[...]
            ys.append(_conv_silu(x, prev_scr[:, cols], cw_ref[:, cols]))
            prev_scr[:, cols] = x[L - SUBLANES:]
        return ys

    @pl.when(s == 0)
    def _():
        prev_scr[...] = hist_ref[0]

    @pl.when(s == 2 * lag)
    def _():
        s_scr[...] = s0_ref[0]

    if lag:
        @pl.when(s == 0)
        def _():
            s_scr[...] = s0_ref[0]
            for scr in (y_scr, u_scr, w_scr, qe_scr, kd_scr, at_scr, gl_scr):
                scr[...] = jnp.zeros_like(scr)

        front = load()
        ys = [y_scr[t] for t in range(3)]
        _gdn_tail(front, z_ref[...], gn_ref[...], s_scr, o_ref, **kw)
        store(_gdn_mid(ys, gb_ref[...], **kw))
        for t, y in enumerate(prep()):
            y_scr[t] = y
    else:
        _gdn_tail(_gdn_mid(prep(), gb_ref[...], **kw), z_ref[...], gn_ref[...], s_scr, o_ref, **kw)

    @pl.when(last)
    def _():
        sfin_ref[0] = s_scr[...]
        conv_ref[0] = prev_scr[...]


def gdn_mix(proj, gb, hist8, cw8, s0, gnorm, base, *, row0, nseq, T, L, H):
    NC = T // L
    tokd = H * HEAD_DIM
    rb0 = row0 // L
    Hs = GDN_STACK
    n = Hs * L
    ng = H // Hs

    lag = 1 if NC > 1 else 0

    def col(group, behind):
        return lambda b, s: (rb0 + b * NC + jnp.clip(s - behind, 0, NC - 1), group)

    base, base_spec, aliases = _row_range_output(base, proj.shape[0], tokd, 9)
    return pl.pallas_call(
        functools.partial(_gdn_kernel, L=L, Hs=Hs, H=H, lag=lag),
        out_shape=(jax.ShapeDtypeStruct((proj.shape[0], tokd), BF16),
                   jax.ShapeDtypeStruct((nseq, H, HEAD_DIM, HEAD_DIM), F32),
                   jax.ShapeDtypeStruct((nseq, SUBLANES, 3 * tokd), F32)),
        grid=(nseq, NC + 2 * lag),
        in_specs=[pl.BlockSpec((L, tokd), col(0, 0)), pl.BlockSpec((L, tokd), col(1, 0)),
                  pl.BlockSpec((L, tokd), col(2, 0)), pl.BlockSpec((L, tokd), col(3, 2 * lag)),
                  pl.BlockSpec((L, LANES), col(0, lag)),
                  pl.BlockSpec((1, SUBLANES, 3 * tokd), lambda b, c: (b, 0, 0)),
                  pl.BlockSpec((SUBLANES, 3 * tokd), lambda b, c: (0, 0)),
                  pl.BlockSpec((1, H, HEAD_DIM, HEAD_DIM), lambda b, c: (b, 0, 0, 0)),
                  pl.BlockSpec((1, HEAD_DIM), lambda b, c: (0, 0)), base_spec],
        out_specs=(pl.BlockSpec((L, tokd), col(0, 2 * lag)),
                   pl.BlockSpec((1, H, HEAD_DIM, HEAD_DIM), lambda b, c: (b, 0, 0, 0)),
                   pl.BlockSpec((1, SUBLANES, 3 * tokd), lambda b, c: (b, 0, 0))),
        scratch_shapes=[pltpu.VMEM((H, HEAD_DIM, HEAD_DIM), F32),
                        pltpu.VMEM((SUBLANES, 3 * tokd), F32), pltpu.VMEM((3, L, tokd), F32),
                        pltpu.VMEM((ng, n, HEAD_DIM), F32), pltpu.VMEM((ng, n, HEAD_DIM), BF16),
                        pltpu.VMEM((ng, n, HEAD_DIM), BF16), pltpu.VMEM((ng, n, HEAD_DIM), BF16),
                        pltpu.VMEM((ng, n, n), BF16), pltpu.VMEM((H, SUBLANES, LANES), F32)],
        input_output_aliases=aliases,
        compiler_params=_params(("parallel", "arbitrary")),
        name="gdn_mix",
    )(proj, proj, proj, proj, gb, hist8, cw8, s0, gnorm.reshape(1, HEAD_DIM), base)


def _cumsum_kernel(x_ref, o_ref, carry_ref, *, tb):
    @pl.when(pl.program_id(0) == 0)
    def _():
        carry_ref[...] = jnp.zeros_like(carry_ref)

    r = lax.broadcasted_iota(jnp.int32, (tb, tb), 0)
    c = lax.broadcasted_iota(jnp.int32, (tb, tb), 1)
    triu = jnp.where(r <= c, 1.0, 0.0).astype(BF16)
    parts = _split3(x_ref[...])
    out = (_dot(parts[0], triu) + _dot(parts[1], triu) + _dot(parts[2], triu)) + carry_ref[...]
    o_ref[...] = out
    carry_ref[...] = out[:, tb - 1:tb]


def cumsum_lanes(x, *, tb):
    R, T = x.shape
    return pl.pallas_call(
        functools.partial(_cumsum_kernel, tb=tb),
        out_shape=jax.ShapeDtypeStruct((R, T), F32),
        grid=(T // tb,),
        in_specs=[pl.BlockSpec((R, tb), lambda i: (0, i))],
        out_specs=pl.BlockSpec((R, tb), lambda i: (0, i)),
        scratch_shapes=[pltpu.VMEM((R, 1), F32)],
        compiler_params=_params(("arbitrary",)),
        name="cumsum_lanes",
    )(x)


def _fox_prompt_kernel(qi_tab, ki_tab, q_ref, gate_ref, k_ref, v_ref, cq_ref, ck_ref, o_ref,
                       cq_scr, m_scr, acc_scr, *, t, hb, rsub):
    hg = pl.program_id(1)
    pair = pl.program_id(2)
    qi = qi_tab[pair]
    ki = ki_tab[pair]
    tr = t // rsub
    hsl = [slice(j * HEAD_DIM, (j + 1) * HEAD_DIM) for j in range(hb)]

    @pl.when(ki == 0)
    def _():
        lane = lax.broadcasted_iota(jnp.int32, cq_ref.shape, 1)
        for j in range(hb):
            col = jnp.sum(jnp.where(lane == hg * hb + j, cq_ref[...], 0.0), axis=1, keepdims=True) * LOG2E
            cq_scr[j] = jnp.broadcast_to(col, (t, LANES))
        m_scr[...] = jnp.full_like(m_scr, -jnp.inf)
        acc_scr[...] = jnp.zeros_like(acc_scr)

    def step(masked):
        probs = [(j, slice(r * tr, (r + 1) * tr), r * tr, (r + 1) * tr if masked else t)
                 for j in range(hb) for r in range(rsub)]
        ck2 = [ck_ref[0, j:j + 1, :] * LOG2E for j in range(hb)]
        v1 = [jnp.concatenate([v_ref[:, hsl[j]], jnp.ones((t, LANES), BF16)], axis=1) for j in range(hb)]
        ss = [_dot_nt(q_ref[rs, hsl[j]], k_ref[:kc, hsl[j]]) for j, rs, _, kc in probs]
        ss = [s * (SCALE * LOG2E) + (jnp.concatenate([cq_scr[j, rs]] * (kc // LANES), axis=1) - ck2[j][:, :kc])
              for s, (j, rs, _, kc) in zip(ss, probs)]
        if masked:
            ss = [jnp.where(lax.broadcasted_iota(jnp.int32, (tr, kc), 1)
                            <= lax.broadcasted_iota(jnp.int32, (tr, kc), 0) + r0, s, -jnp.inf)
                  for s, (_, _, r0, kc) in zip(ss, probs)]
        m_olds = [m_scr[j, rs] for j, rs, _, _ in probs]
        m_news = [jnp.maximum(mo, jnp.max(s, axis=-1, keepdims=True)) for mo, s in zip(m_olds, ss)]
        ps = [jnp.exp2(s - jnp.concatenate([mn] * (kc // LANES), axis=1)).astype(BF16)
              for s, mn, (_, _, _, kc) in zip(ss, m_news, probs)]
        pvs = [_dot(p, v1[j][:kc]) for p, (j, _, _, kc) in zip(ps, probs)]
        for (j, rs, _, _), mo, mn, pv in zip(probs, m_olds, m_news, pvs):
            alpha = jnp.exp2(mo - mn)
            acc_scr[j, rs] = jnp.concatenate([alpha, alpha], axis=1) * acc_scr[j, rs] + pv
            m_scr[j, rs] = mn

    @pl.when(ki < qi)
    def _():
        step(False)

    @pl.when(ki == qi)
    def _():
        step(True)
        for j in range(hb):
            acc = acc_scr[j]
            o = acc[:, :HEAD_DIM] / acc[:, HEAD_DIM:HEAD_DIM + 1]
            o_ref[:, hsl[j]] = (o * _sigmoid(gate_ref[:, hsl[j]].astype(F32))).astype(o_ref.dtype)


def fox_prompt(proj, karr, varr, c_col, c_rows, *, nseq, T, H, t, hb, rsub):
    nb = T // t
    ng = H // hb
    W = hb * HEAD_DIM
    pairs = [(qi, ki) for qi in range(nb) for ki in range(qi + 1)]
    qi_tab = jnp.asarray([p[0] for p in pairs], jnp.int32)
    ki_tab = jnp.asarray([p[1] for p in pairs], jnp.int32)
    grid_spec = pltpu.PrefetchScalarGridSpec(
        num_scalar_prefetch=2,
        grid=(nseq, ng, len(pairs)),
        in_specs=[pl.BlockSpec((t, W), lambda b, g, p, qt, kt: (b * nb + qt[p], g)),
                  pl.BlockSpec((t, W), lambda b, g, p, qt, kt: (b * nb + qt[p], ng + g)),
                  pl.BlockSpec((t, W), lambda b, g, p, qt, kt: (b * nb + kt[p], g)),
                  pl.BlockSpec((t, W), lambda b, g, p, qt, kt: (b * nb + kt[p], g)),
                  pl.BlockSpec((t, H), lambda b, g, p, qt, kt: (b * nb + qt[p], 0)),
                  pl.BlockSpec((1, hb, t), lambda b, g, p, qt, kt: (b * ng + g, 0, kt[p]))],
        out_specs=pl.BlockSpec((t, W), lambda b, g, p, qt, kt: (b * nb + qt[p], g)),
        scratch_shapes=[pltpu.VMEM((hb, t, LANES), F32), pltpu.VMEM((hb, t, LANES), F32),
                        pltpu.VMEM((hb, t, 2 * HEAD_DIM), F32)])
    return pl.pallas_call(
        functools.partial(_fox_prompt_kernel, t=t, hb=hb, rsub=rsub),
        out_shape=jax.ShapeDtypeStruct((proj.shape[0], H * HEAD_DIM), BF16),
        grid_spec=grid_spec,
        compiler_params=_params(("parallel", "parallel", "arbitrary")),
        name="fox_prompt",
    )(qi_tab, ki_tab, proj, proj, karr, varr, c_col, c_rows.reshape(nseq * ng, hb, T))


def _fox_sample_kernel(q_ref, gate_ref, ck_ref, cv_ref, kn_ref, vn_ref, cq_ref, cc_ref, cn_ref, base_ref, o_ref,
                       m_scr, l_scr, acc_scr, *, H, T):
    del base_ref
    ki = pl.program_id(1)

    @pl.when(ki == 0)
    def _():
        m_scr[...] = jnp.full_like(m_scr, -jnp.inf)
        l_scr[...] = jnp.zeros_like(l_scr)
        acc_scr[...] = jnp.zeros_like(acc_scr)

    cq = cq_ref[...]

    sls = [slice(h * HEAD_DIM, (h + 1) * HEAD_DIM) for h in range(H)]

    def update(k_of, v_of, ck_of, mask):
        ss = [_dot_nt(q_ref[:, sl], k_of(h)) * SCALE + (cq[:, h:h + 1] - ck_of(h)) for h, sl in enumerate(sls)]
        if mask is not None:
            ss = [jnp.where(mask, s, -jnp.inf) for s in ss]
        m_olds = [m_scr[h] for h in range(H)]
        m_news = [jnp.maximum(mo, jnp.max(s, axis=-1, keepdims=True)) for mo, s in zip(m_olds, ss)]
        alphas = [jnp.exp(mo - mn) for mo, mn in zip(m_olds, m_news)]
        ps = [jnp.exp(s - mn) for s, mn in zip(ss, m_news)]
        pvs = [_dot(p.astype(BF16), v_of(h)) for h, p in enumerate(ps)]
        for h, sl in enumerate(sls):
            l_scr[h] = alphas[h] * l_scr[h] + jnp.sum(ps[h], axis=-1, keepdims=True)
            acc_scr[:, sl] = alphas[h] * acc_scr[:, sl] + pvs[h]
            m_scr[h] = m_news[h]

    update(lambda h: ck_ref[0, h].astype(BF16), lambda h: cv_ref[0, h].astype(BF16),
           lambda h: cc_ref[0, h:h + 1, :], None)

    @pl.when(ki == pl.num_programs(1) - 1)
    def _():
        causal = (lax.broadcasted_iota(jnp.int32, (T, T), 1) <= lax.broadcasted_iota(jnp.int32, (T, T), 0))
        update(lambda h: kn_ref[:, sls[h]], lambda h: vn_ref[:, sls[h]], lambda h: cn_ref[0, h:h + 1, :], causal)
        for h, sl in enumerate(sls):
            o = acc_scr[:, sl] / l_scr[h] * _sigmoid(gate_ref[:, sl].astype(F32))
            o_ref[:, sl] = o.astype(o_ref.dtype)


def fox_sample(proj, cache_k, cache_v, k_new, v_new, c_col, c_cache, c_new, base, *, row0, nseq, T, H, tk):
    P = cache_k.shape[2]
    W = H * HEAD_DIM
    rb0 = row0 // T
    cache_spec = pl.BlockSpec((1, H, tk, HEAD_DIM), lambda b, i: (b, 0, i, 0))
    base, base_spec, aliases = _row_range_output(base, proj.shape[0], W, 9)
    new_spec = pl.BlockSpec((T, W), lambda b, i: (b, 0))
    return pl.pallas_call(
        functools.partial(_fox_sample_kernel, H=H, T=T),
        out_shape=jax.ShapeDtypeStruct((proj.shape[0], W), BF16),
        grid=(nseq, P // tk),
        in_specs=[pl.BlockSpec((T, W), lambda b, i: (rb0 + b, 0)),
                  pl.BlockSpec((T, W), lambda b, i: (rb0 + b, 1)),
                  cache_spec, cache_spec, new_spec, new_spec,
                  pl.BlockSpec((T, H), lambda b, i: (b, 0)),
                  pl.BlockSpec((1, H, tk), lambda b, i: (b, 0, i)),
                  pl.BlockSpec((1, H, T), lambda b, i: (b, 0, 0)), base_spec],
        out_specs=pl.BlockSpec((T, W), lambda b, i: (rb0 + b, 0)),
        scratch_shapes=[pltpu.VMEM((H, T, 1), F32), pltpu.VMEM((H, T, 1), F32), pltpu.VMEM((T, W), F32)],
        input_output_aliases=aliases,
        compiler_params=_params(("parallel", "arbitrary")),
        name="fox_sample",
    )(proj, proj, cache_k, cache_v, k_new, v_new, c_col, c_cache, c_new, base)


def _pad_cols(a, width):
    return jnp.pad(a, ((0, 0), (0, width - a.shape[1])))


def kernel(x_prompt, x_sample, state_gdn, state_conv, cache_k, cache_v, cache_logf, cache_mem_k, cache_mem_v, mem_prompt, norm_mix_pre, norm_mix_post, norm_mlp_pre, norm_mlp_post, w_in_a, conv_w_a, a_log, dt_bias, gdn_norm, w_in_b, norm_kv, w_kvf, b_f, norm_mem, w_mem_kv, w_o, w_up, w_down):
    Bp, Tp, D = x_prompt.shape
    Bs, Ts, _ = x_sample.shape
    n_a = w_in_a.shape[0]
    depth = w_o.shape[0]
    H = a_log.shape[1]
    tokd = H * HEAD_DIM
    qkvd = 3 * tokd
    P = cache_k.shape[1]
    nmem = mem_prompt.shape[1]
    mh = cache_mem_k.shape[3]
    memd = mh * HEAD_DIM
    Mp, Ms = Bp * Tp, Bs * Ts
    M = Mp + Ms
    TM = next(t for t in (1024, 512, 256, 128, 64, 32) if M % t == 0)
    TMH = max(TM // 2, 32)
    HR = -(-H // SUBLANES) * SUBLANES

    x = jnp.concatenate([x_prompt.reshape(Mp, D), x_sample.reshape(Ms, D)], axis=0)

    mem_rows = mem_prompt.reshape(Bp * nmem, D)
    wkv = w_mem_kv.astype(BF16)
    pkv = jnp.stack([norm_matmul(mem_rows, norm_mem[l], wkv, layer=l, tm=min(TM, Bp * nmem), tn=memd)
                     for l in range(depth)])
    p_mem_k = pkv[:, :, :memd].reshape(depth, Bp, nmem, mh, HEAD_DIM)
    p_mem_v = pkv[:, :, memd:].reshape(depth, Bp, nmem, mh, HEAD_DIM)
    wo_all = w_o.astype(BF16)
    wup_all = w_up.astype(BF16)
    wdn_all = w_down.astype(BF16)
    wb_all = w_in_b.astype(BF16)

    def hist8(h):
        return jnp.pad(h, ((0, 0), (SUBLANES - (CONV_W - 1), 0), (0, 0)))

    cache_k2 = cache_k.transpose(0, 2, 1, 3)
    cache_v2 = cache_v.transpose(0, 2, 1, 3)

    p_gdn, s_gdn, p_conv, s_conv = [], [], [], []
    lane = jnp.arange(LANES)
    for l in range(depth):
        if l == n_a:
            wk = w_kvf[:, :tokd].astype(BF16)
            wv = w_kvf[:, tokd:2 * tokd].astype(BF16)
            wf = _pad_cols(w_kvf[:, 2 * tokd:], LANES).astype(BF16)
            kp2, p_k = kv_proj(x, norm_kv, wk, row0=0, nseq=Bp, T=Tp, tm=TMH, H=H)
            vp2, p_v = kv_proj(x, norm_kv, wv, row0=0, nseq=Bp, T=Tp, tm=TMH, H=H)
            ks2, s_k = kv_proj(x, norm_kv, wk, row0=Mp, nseq=Bs, T=Ts, tm=TMH, H=H)
            vs2, s_v = kv_proj(x, norm_kv, wv, row0=Mp, nseq=Bs, T=Ts, tm=TMH, H=H)
            pf = jnp.zeros((SUBLANES, LANES), F32).at[0, :H].set(b_f)
            logf = norm_matmul(x, norm_kv, wf[None], tm=TM, tn=LANES, epilogue=_ep_log_forget, p=pf)[:, :H]
            lf_p = logf[:Mp].reshape(Bp, Tp, H)
            lf_s = logf[Mp:].reshape(Bs, Ts, H)

            def rows_of(a):
                a = jnp.pad(a.transpose(0, 2, 1), ((0, 0), (0, HR - H), (0, 0)))
                return a.reshape(a.shape[0] * HR, a.shape[2])

            cp_rows = cumsum_lanes(rows_of(lf_p), tb=min(Tp, 512)).reshape(Bp, HR, Tp)[:, :H]
            cp_col = cp_rows.transpose(0, 2, 1).reshape(Mp, H)
            tot = P + Ts
            tot_pad = -(-tot // LANES) * LANES
            lf_all = jnp.pad(jnp.concatenate([cache_logf, lf_s], axis=1), ((0, 0), (0, tot_pad - tot), (0, 0)))
            cs_rows = cumsum_lanes(rows_of(lf_all), tb=LANES).reshape(Bs, HR, tot_pad)[:, :H]
            cs_cache = cs_rows[:, :, :P]
            cs_new = cs_rows[:, :, P:tot]
            cs_col = cs_new.transpose(0, 2, 1).reshape(Ms, H)

        if l < n_a:
            w = w_in_a[l]
            o1 = qkvd + tokd
            w_main = jnp.concatenate([w[:, :o1], w[:, o1 + 2 * H:]], axis=1).astype(BF16)
            w_ab = _pad_cols(w[:, o1:o1 + 2 * H], LANES).astype(BF16)
            pg = jnp.zeros((SUBLANES, LANES), F32)
            pg = pg.at[0, :H].set(a_log[l]).at[1, :H].set(dt_bias[l]).at[2].set((lane < H).astype(F32))
            proj, gb = norm_matmul_side(x, norm_mix_pre[l], w_main, w_ab, pg, _ep_gdn_gates,
                                        tm=TM, tn=_col_tile(w_main.shape[1]))
            cw8 = jnp.pad(conv_w_a[l], ((0, SUBLANES - CONV_W), (0, 0)))
            tok, sp, cp = gdn_mix(proj, gb, hist8(jnp.zeros((Bp, CONV_W - 1, qkvd), F32)), cw8,
                                  jnp.zeros((Bp, H, HEAD_DIM, HEAD_DIM), F32), gdn_norm[l], None,
                                  row0=0, nseq=Bp, T=Tp, L=min(Tp, GDN_CHUNK), H=H)
            tok, ss, cs = gdn_mix(proj, gb, hist8(state_conv[l]), cw8, state_gdn[l], gdn_norm[l], tok,
                                  row0=Mp, nseq=Bs, T=Ts, L=min(Ts, GDN_CHUNK), H=H)
            p_gdn.append(sp)
            s_gdn.append(ss)
            p_conv.append(cp[:, SUBLANES - (CONV_W - 1):])
            s_conv.append(cs[:, SUBLANES - (CONV_W - 1):])
            mq_block = (qkvd + tokd) // memd
        else:
            proj = norm_matmul(x, norm_mix_pre[l], wb_all, layer=l - n_a, tm=TM, tn=_col_tile(wb_all.shape[2]),
                               out_dtype=BF16)
            tfox = min(Tp, 1024)
            tok = fox_prompt(proj, kp2, vp2, cp_col, cp_rows, nseq=Bp, T=Tp, H=H, t=tfox,
                             hb=2, rsub=max(tfox // 256, 1))
            tok = fox_sample(proj, cache_k2, cache_v2, ks2, vs2, cs_col, cs_cache, cs_new, tok,
                             row0=Mp, nseq=Bs, T=Ts, H=H, tk=min(P, 1024))
            mq_block = 2 * tokd // memd
        mem_o = mem_attend(proj, p_mem_k, p_mem_v, l, None, row0=0, rows_per_seq=Tp, tm=min(Tp, 512), sb=1,
                           col_block=mq_block)
        mem_o = mem_attend(proj, cache_mem_k, cache_mem_v, l, mem_o, row0=Mp, rows_per_seq=Ts, tm=Ts,
                           sb=math.gcd(Bs, 8), col_block=mq_block)
        x = out_proj(tok, mem_o, x, wo_all, l, norm_mix_post[l], tm=TMH)
        x = mlp(x, norm_mlp_pre[l], norm_mlp_post[l], wup_all, wdn_all, l, tm=TMH, tf=min(wup_all.shape[2], 1024))

    y_prompt = x[:Mp].reshape(Bp, Tp, D)
    y_sample = x[Mp:].reshape(Bs, Ts, D)
    return (y_prompt, y_sample, jnp.stack(p_gdn), jnp.stack(p_conv),
            p_k.transpose(0, 2, 1, 3), p_v.transpose(0, 2, 1, 3), lf_p, p_mem_k, p_mem_v,
            jnp.stack(s_gdn), jnp.stack(s_conv),
            s_k.transpose(0, 2, 1, 3), s_v.transpose(0, 2, 1, 3), lf_s)
```

```python
import functools
import math

import jax
import jax.numpy as jnp
from jax import lax
from jax.experimental import pallas as pl
from jax.experimental.pallas import tpu as pltpu

F32 = jnp.float32
BF16 = jnp.bfloat16
EPS = 1e-6
HEAD_DIM = 128
LANES = 128
SUBLANES = 8
SCALE = HEAD_DIM ** -0.5
LOG2E = math.log2(math.e)
CONV_W = 4
GDN_CHUNK = 64
GDN_STACK = 2
SOLVE_BLOCK = 16
V7X_VMEM_BYTES = 64 * 1024 * 1024
VMEM_LIMIT = V7X_VMEM_BYTES * 7 // 8
MAX_COL_TILE = 1792


def _params(sem):
    return pltpu.CompilerParams(dimension_semantics=sem, vmem_limit_bytes=VMEM_LIMIT)


def _rms(x):
    return x * lax.rsqrt(jnp.mean(x * x, axis=-1, keepdims=True) + EPS)


def _sigmoid(x):
    return 1.0 / (1.0 + jnp.exp(-x))


def _softplus(x):
    return jnp.maximum(x, 0.0) + jnp.log1p(jnp.exp(-jnp.abs(x)))


def _dot(a, b):
    return jnp.dot(a, b, preferred_element_type=F32)


def _dot_tn(a, b):
    return lax.dot_general(a, b, (((0,), (0,)), ((), ())), preferred_element_type=F32)


def _dot_nt(a, b):
    return lax.dot_general(a, b, (((1,), (1,)), ((), ())), preferred_element_type=F32)


def _split3(x):
    hi = x.astype(BF16)
    r = x - hi.astype(F32)
    mid = r.astype(BF16)
    return hi, mid, (r - mid.astype(F32)).astype(BF16)


def _ep_gdn_gates(acc, p):
    g = -jnp.exp(p[0:1]) * _softplus(acc + p[1:2])
    return jnp.where(p[2:3] > 0.5, g, _sigmoid(acc))


def _ep_log_forget(acc, p):
    return -_softplus(-(acc + p[0:1]))


def _norm_matmul_kernel(x_ref, g_ref, w_ref, o_ref, h_ref):
    @pl.when(pl.program_id(1) == 0)
    def _():
        h_ref[...] = (_rms(x_ref[...]) * g_ref[...]).astype(BF16)

    o_ref[...] = _dot(h_ref[...], w_ref[...]).astype(o_ref.dtype)


def _norm_matmul_side_kernel(x_ref, g_ref, w_ref, ws_ref, ps_ref, o_ref, os_ref, h_ref, *, side_epilogue):
    @pl.when(pl.program_id(1) == 0)
    def _():
        h = (_rms(x_ref[...]) * g_ref[...]).astype(BF16)
        h_ref[...] = h
        os_ref[...] = side_epilogue(_dot(h, ws_ref[...]), ps_ref[...])

    o_ref[...] = _dot(h_ref[...], w_ref[...]).astype(o_ref.dtype)


def norm_matmul_side(x, g, w, w_side, p_side, side_epilogue, *, tm, tn):
    M, K = x.shape
    N = w.shape[1]
    return pl.pallas_call(
        functools.partial(_norm_matmul_side_kernel, side_epilogue=side_epilogue),
        out_shape=(jax.ShapeDtypeStruct((M, N), F32), jax.ShapeDtypeStruct((M, LANES), F32)),
        grid=(M // tm, N // tn),
        in_specs=[pl.BlockSpec((tm, K), lambda i, j: (i, 0)),
                  pl.BlockSpec((1, K), lambda i, j: (0, 0)),
                  pl.BlockSpec((K, tn), lambda i, j: (0, j)),
                  pl.BlockSpec((K, LANES), lambda i, j: (0, 0)),
                  pl.BlockSpec((SUBLANES, LANES), lambda i, j: (0, 0))],
        out_specs=(pl.BlockSpec((tm, tn), lambda i, j: (i, j)),
                   pl.BlockSpec((tm, LANES), lambda i, j: (i, 0))),
        scratch_shapes=[pltpu.VMEM((tm, K), BF16)],
        compiler_params=_params(("parallel", "arbitrary")),
        name="norm_matmul_side",
    )(x, g.reshape(1, K), w, w_side, p_side)


def _col_tile(n):
    return max(c for c in range(LANES, min(n, MAX_COL_TILE) + 1, LANES) if n % c == 0)


def norm_matmul(x, g, w, *, tm, tn, layer, out_dtype):
    M, K = x.shape
    N = w.shape[2]
    return pl.pallas_call(
        _norm_matmul_kernel,
        out_shape=jax.ShapeDtypeStruct((M, N), out_dtype),
        grid=(M // tm, N // tn),
        in_specs=[pl.BlockSpec((tm, K), lambda i, j: (i, 0)),
                  pl.BlockSpec((1, K), lambda i, j: (0, 0)),
                  pl.BlockSpec((None, K, tn), lambda i, j: (layer, 0, j))],
        out_specs=pl.BlockSpec((tm, tn), lambda i, j: (i, j)),
        scratch_shapes=[pltpu.VMEM((tm, K), BF16)],
        compiler_params=_params(("parallel", "arbitrary")),
        name="norm_matmul",
    )(x, g.reshape(1, K), w)


def _kv_proj_kernel(x_ref, g_ref, w_ref, pf_ref, o2_ref, o4_ref, *lf_ref, H, ns, tt):
    h = (_rms(x_ref[...]) * g_ref[...]).astype(BF16)
    acc = _dot(h, w_ref[...])
    tokd = H * HEAD_DIM
    o2_ref[...] = acc[:, :tokd].astype(o2_ref.dtype)
    for s in range(ns):
        for hh in range(H):
            o4_ref[s, hh] = acc[s * tt:(s + 1) * tt, hh * HEAD_DIM:(hh + 1) * HEAD_DIM]
    if lf_ref:
        lf_ref[0][...] = _ep_log_forget(acc[:, tokd:], pf_ref[...])


def kv_proj(x, g, w, pf, *, row0, nseq, T, tm, H):
    K = x.shape[1]
    N = w.shape[1]
    tokd = H * HEAD_DIM
    with_logf = N == tokd + LANES
    assert with_logf or N == tokd
    rows = nseq * T
    rb0 = row0 // tm
    ns, tt = (tm // T, T) if tm >= T else (1, tm)
    parts = T // tt
    return pl.pallas_call(
        functools.partial(_kv_proj_kernel, H=H, ns=ns, tt=tt),
        out_shape=(jax.ShapeDtypeStruct((rows, tokd), BF16),
                   jax.ShapeDtypeStruct((nseq, H, T, HEAD_DIM), F32))
        + ((jax.ShapeDtypeStruct((rows, LANES), F32),) if with_logf else ()),
        grid=(rows // tm,),
        in_specs=[pl.BlockSpec((tm, K), lambda i: (rb0 + i, 0)),
                  pl.BlockSpec((1, K), lambda i: (0, 0)),
                  pl.BlockSpec((K, N), lambda i: (0, 0)),
                  pl.BlockSpec((SUBLANES, LANES), lambda i: (0, 0))],
        out_specs=(pl.BlockSpec((tm, tokd), lambda i: (i, 0)),
                   pl.BlockSpec((ns, H, tt, HEAD_DIM), lambda i: (i // parts, 0, i % parts, 0)))
        + ((pl.BlockSpec((tm, LANES), lambda i: (i, 0)),) if with_logf else ()),
        compiler_params=_params(("parallel",)),
        name="kv_proj",
    )(x, g.reshape(1, K), w, pf)


def _out_proj_kernel(tok_ref, mem_ref, x_ref, wa_ref, wb_ref, g_ref, o_ref):
    mix = _dot(tok_ref[...], wa_ref[...]) + _dot(mem_ref[...], wb_ref[...])
    o_ref[...] = x_ref[...] + _rms(mix) * g_ref[...]


def out_proj(tok, mem, x, w, layer, g, *, tm):
    M, D = x.shape
    Ka, Kb = tok.shape[1], mem.shape[1]
    assert Ka % Kb == 0
    return pl.pallas_call(
        _out_proj_kernel,
        out_shape=jax.ShapeDtypeStruct((M, D), F32),
        grid=(M // tm,),
        in_specs=[pl.BlockSpec((tm, Ka), lambda i: (i, 0)),
                  pl.BlockSpec((tm, Kb), lambda i: (i, 0)),
                  pl.BlockSpec((tm, D), lambda i: (i, 0)),
                  pl.BlockSpec((None, Ka, D), lambda i: (layer, 0, 0)),
                  pl.BlockSpec((None, Kb, D), lambda i: (layer, Ka // Kb, 0)),
                  pl.BlockSpec((1, D), lambda i: (0, 0))],
        out_specs=pl.BlockSpec((tm, D), lambda i: (i, 0)),
        compiler_params=_params(("parallel",)),
        name="out_proj",
    )(tok, mem, x, w, w, g.reshape(1, D))


def _mlp_kernel(x_ref, gpre_ref, gpost_ref, wup_ref, wdn_ref, *rest, split):
    outs, (h_ref, acc_ref) = rest[:-2], rest[-2:]
    j = pl.program_id(1)

    @pl.when(j == 0)
    def _():
        h_ref[...] = (_rms(x_ref[...]) * gpre_ref[...]).astype(BF16)
        acc_ref[...] = jnp.zeros_like(acc_ref)

    up = _dot(h_ref[...], wup_ref[...])
    act = jnp.square(jnp.maximum(up, 0.0)).astype(BF16)
    acc_ref[...] += _dot(act, wdn_ref[...])

    def result():
        return x_ref[...] + _rms(acc_ref[...]) * gpost_ref[...]

    last = j == pl.num_programs(1) - 1
    if split is None:
        @pl.when(last)
        def _():
            outs[0][...] = result()
    else:
        first_group = pl.program_id(0) < split

        @pl.when(last & first_group)
        def _():
            outs[0][...] = result()

        @pl.when(last & jnp.logical_not(first_group))
        def _():
            outs[1][...] = result()


def mlp(x, gpre, gpost, wup, wdn, layer, *, tm, tf, split_rows=None):
    M, D = x.shape
    FF = wup.shape[2]
    if split_rows is None:
        split = None
        out_shape = jax.ShapeDtypeStruct((M, D), F32)
        out_specs = pl.BlockSpec((tm, D), lambda i, j: (i, 0))
    else:
        split = split_rows // tm
        assert split_rows % tm == 0 and 0 < split < M // tm
        out_shape = (jax.ShapeDtypeStruct((split_rows, D), F32), jax.ShapeDtypeStruct((M - split_rows, D), F32))
        out_specs = (pl.BlockSpec((tm, D), lambda i, j: (jnp.minimum(i, split - 1), 0)),
                     pl.BlockSpec((tm, D), lambda i, j: (jnp.maximum(i - split, 0), 0)))
    return pl.pallas_call(
        functools.partial(_mlp_kernel, split=split),
        out_shape=out_shape,
        grid=(M // tm, FF // tf),
        in_specs=[pl.BlockSpec((tm, D), lambda i, j: (i, 0)),
                  pl.BlockSpec((1, D), lambda i, j: (0, 0)),
                  pl.BlockSpec((1, D), lambda i, j: (0, 0)),
                  pl.BlockSpec((None, D, tf), lambda i, j: (layer, 0, j)),
                  pl.BlockSpec((None, tf, D), lambda i, j: (layer, j, 0))],
        out_specs=out_specs,
        scratch_shapes=[pltpu.VMEM((tm, D), BF16), pltpu.VMEM((tm, D), F32)],
        compiler_params=_params(("arbitrary", "arbitrary")),
        name="mlp",
    )(x, gpre.reshape(1, D), gpost.reshape(1, D), wup, wdn)


def _row_range_output(base, total_rows, width, n_inputs):
    spec = pl.BlockSpec(memory_space=pl.ANY)
    if base is None:
        return jnp.zeros((SUBLANES, LANES), BF16), spec, {}
    assert base.shape == (total_rows, width) and base.dtype == BF16
    return base, spec, {n_inputs: 0}


def _mem_attn_kernel(q_ref, mk_ref, mv_ref, base_ref, o_ref, *, heads, sb, tm):
    del base_ref
    probs = [(s, slice(s * tm, (s + 1) * tm), h, slice(h * HEAD_DIM, (h + 1) * HEAD_DIM))
             for s in range(sb) for h in range(heads)]
    ss = [_dot_nt(q_ref[rs, sl].astype(BF16), mk_ref[0, s, :, h, :].astype(BF16)) * SCALE
          for s, rs, h, sl in probs]
    es = [jnp.exp(s - jnp.max(s, axis=-1, keepdims=True)) for s in ss]
    ps = [(e / jnp.sum(e, axis=-1, keepdims=True)).astype(BF16) for e in es]
    outs = [_dot(p, mv_ref[0, s, :, h, :].astype(BF16)) for p, (s, _, h, _) in zip(ps, probs)]
    for o, (_, rs, _, sl) in zip(outs, probs):
        o_ref[rs, sl] = o.astype(o_ref.dtype)


def mem_attend(proj, mk, mv, layer, base, *, row0, rows_per_seq, tm, sb, col_block):
    _, nseq, nmem, heads, _ = mk.shape
    assert sb == 1 or tm == rows_per_seq
    W = heads * HEAD_DIM
    nt = rows_per_seq // tm
    rb0 = row0 // (sb * tm)
    mem_spec = pl.BlockSpec((1, sb, nmem, heads, HEAD_DIM), lambda b, i: (layer, b, 0, 0, 0))
    base, base_spec, aliases = _row_range_output(base, proj.shape[0], W, 3)
    return pl.pallas_call(
        functools.partial(_mem_attn_kernel, heads=heads, sb=sb, tm=tm),
        out_shape=jax.ShapeDtypeStruct((proj.shape[0], W), BF16),
        grid=(nseq // sb, nt),
        in_specs=[pl.BlockSpec((sb * tm, W), lambda b, i: (rb0 + b * nt + i, col_block)), mem_spec, mem_spec,
                  base_spec],
        out_specs=pl.BlockSpec((sb * tm, W), lambda b, i: (rb0 + b * nt + i, 0)),
        input_output_aliases=aliases,
        compiler_params=_params(("parallel", "parallel")),
        name="mem_attend",
    )(proj, mk, mv, base)


def _bdot(a, b):
    return _dot(a.astype(BF16), b.astype(BF16))


def _unit_lower_solve(As, rhss, n, L):
    row = lax.broadcasted_iota(jnp.int32, (n, n), 0)
    col = lax.broadcasted_iota(jnp.int32, (n, n), 1)
    shift = SOLVE_BLOCK.bit_length() - 1
    same = (row >> shift) == (col >> shift)
    es = [jnp.where(same, -A, 0.0) for A in As]
    offs = [jnp.where(same, 0.0, A) for A in As]
    ps = es
    span = 2
    while span < SOLVE_BLOCK:
        ps = [_bdot(p, p) for p in ps]
        es = [e + p + _bdot(e, p) for e, p in zip(es, ps)]
        span *= 2
    ys = [r + _bdot(e, r) for e, r in zip(es, rhss)]
    nblocks = L // SOLVE_BLOCK
    if nblocks > 1:
        powers = [[o + _bdot(e, o) for e, o in zip(es, offs)]]
        span = 2
        while span < nblocks:
            powers.append([_bdot(p, p) for p in powers[-1]])
            span *= 2
        for pw in reversed(powers[1:]):
            ys = [y + _bdot(p, y) for p, y in zip(pw, ys)]
        ys = [y - _bdot(p, y) for p, y in zip(powers[0], ys)]
    return ys


def _stack_heads(t, heads):
    return jnp.concatenate([t[:, h * HEAD_DIM:(h + 1) * HEAD_DIM] for h in heads], axis=0)


def _conv_silu(x, prev, w):
    row8 = lax.broadcasted_iota(jnp.int32, (SUBLANES, 1), 0)
    y = None
    for s in range(CONV_W - 1, 0, -1):
        sh = pltpu.roll(x, s, 0)
        top = jnp.where(row8 >= s, sh[:SUBLANES], pltpu.roll(prev, s, 0))
        term = jnp.concatenate([top, sh[SUBLANES:]], axis=0) * w[CONV_W - 1 - s:CONV_W - s]
        y = term if y is None else y + term
    y = y + x * w[CONV_W - 1:CONV_W]
    return y * _sigmoid(y)


def _gdn_mid(ys, gb, *, L, Hs, H):
    n = Hs * L
    lshift = L.bit_length() - 1
    yq, yk, yv = ys

    r2 = lax.broadcasted_iota(jnp.int32, (n, n), 0)
    c2 = lax.broadcasted_iota(jnp.int32, (n, n), 1)
    same = (r2 >> lshift) == (c2 >> lshift)
    incl = same & (r2 >= c2)
    strict = same & (r2 > c2)
    stack = _stack_heads
    groups = [[gi * Hs + j for j in range(Hs)] for gi in range(H // Hs)]

    rl = lax.broadcasted_iota(jnp.int32, (L, L), 0)
    cl = lax.broadcasted_iota(jnp.int32, (L, L), 1)
    tril = jnp.where(rl >= cl, 1.0, 0.0).astype(BF16)
    gcum = sum(_dot(tril, part) for part in _split3(gb))
    gcum_t = jnp.concatenate([gcum, jnp.zeros((LANES - L, LANES), F32)], axis=0).T if L < LANES else gcum.T

    def col(tile, heads, off=0):
        return jnp.concatenate([tile[:, off + h:off + h + 1] for h in heads], axis=0)

    Gs = [col(gcum, hs) for hs in groups]
    Grs = [jnp.concatenate([gcum_t[h:h + 1, :L] for h in hs], axis=1) for hs in groups]
    betas = [col(gb, hs, H) for hs in groups]
    glasts = [[gcum[L - 1:L, h:h + 1] for h in hs] for hs in groups]
    GLs = [jnp.concatenate([jnp.broadcast_to(g, (L, 1)) for g in gl], axis=0) for gl in glasts]

    def l2n(t):
        return t * lax.rsqrt(jnp.sum(t * t, axis=-1, keepdims=True) + EPS)

    qs = [l2n(stack(yq, hs)) * SCALE for hs in groups]
    ks = [l2n(stack(yk, hs)) for hs in groups]
    vs = [stack(yv, hs) for hs in groups]
    eGs = [jnp.exp(G) for G in Gs]
    decays = [jnp.exp(jnp.where(incl, G - Gr, -jnp.inf)) for G, Gr in zip(Gs, Grs)]
    kbs = [k.astype(BF16) for k in ks]
    kks = [_dot_nt(kb, kb) for kb in kbs]
    qks = [_dot_nt(q.astype(BF16), kb) for q, kb in zip(qs, kbs)]
    As = [jnp.where(strict, b * kk * d, 0.0) for b, kk, d in zip(betas, kks, decays)]
    rhss = [jnp.concatenate([b * v, (b * eG) * k], axis=-1) for b, v, eG, k in zip(betas, vs, eGs, ks)]
    sols = _unit_lower_solve(As, rhss, n, L)
    us = [s[:, :HEAD_DIM] for s in sols]
    ws = [s[:, HEAD_DIM:].astype(BF16) for s in sols]
    qes = [(q * eG).astype(BF16) for q, eG in zip(qs, eGs)]
    kds = [(k * jnp.exp(GL - G)).astype(BF16) for k, GL, G in zip(ks, GLs, Gs)]
    attns = [(qk * d).astype(BF16) for qk, d in zip(qks, decays)]
    return us, ws, qes, kds, attns, glasts


def _gdn_tail(front, z, gnorm, s_scr, o_ref, *, L, Hs, H):
    us, ws, qes, kds, attns, glasts = front
    groups = [[gi * Hs + j for j in range(Hs)] for gi in range(H // Hs)]
    rsl = [slice(j * L, (j + 1) * L) for j in range(Hs)]
    sbs = [[s_scr[h].astype(BF16) for h in hs] for hs in groups]
    v_news = [jnp.concatenate([u[rs] - _dot(w[rs], sb) for rs, sb in zip(rsl, sbg)], axis=0).astype(BF16)
              for u, w, sbg in zip(us, ws, sbs)]
    outs = [jnp.concatenate([_dot(qe[rs], sb) for rs, sb in zip(rsl, sbg)], axis=0) + _dot(at, vn)
            for qe, sbg, at, vn in zip(qes, sbs, attns, v_news)]
    for hs, gl, kd, vn in zip(groups, glasts, kds, v_news):
        for j, h in enumerate(hs):
            s_scr[h] = jnp.exp(gl[j]) * s_scr[h] + _dot_tn(kd[rsl[j]], vn[rsl[j]])
    for hs, out in zip(groups, outs):
        zz = _stack_heads(z, hs)
        o = (_rms(out) * gnorm * (zz * _sigmoid(zz))).astype(o_ref.dtype)
        for j, h in enumerate(hs):
            o_ref[:, h * HEAD_DIM:(h + 1) * HEAD_DIM] = o[rsl[j]]


def _gdn_kernel(q_ref, k_ref, v_ref, z_ref, gb_ref, hist_ref, cw_ref, s0_ref, gn_ref, base_ref,
                o_ref, sfin_ref, conv_ref,
                s_scr, prev_scr, y_scr, u_scr, w_scr, qe_scr, kd_scr, at_scr, gl_scr, *, L, Hs, H, lag):
    del base_ref
    s = pl.program_id(1)
    last = s == pl.num_programs(1) - 1
    tokd = H * HEAD_DIM
    kw = dict(L=L, Hs=Hs, H=H)
    ngroups = H // Hs

    def store(front):
        us, ws, qes, kds, attns, glasts = front
        for g in range(ngroups):
            u_scr[g], w_scr[g], qe_scr[g], kd_scr[g], at_scr[g] = us[g], ws[g], qes[g], kds[g], attns[g]
            for j, gl in enumerate(glasts[g]):
                gl_scr[g * Hs + j] = jnp.broadcast_to(gl, (SUBLANES, LANES))

    def load():
        rng = range(ngroups)
        glasts = [[gl_scr[g * Hs + j][0:1, 0:1] for j in range(Hs)] for g in rng]
        return ([u_scr[g] for g in rng], [w_scr[g] for g in rng], [qe_scr[g] for g in rng],
                [kd_scr[g] for g in rng], [at_scr[g] for g in rng], glasts)

    def prep():
        ys = []
        for t, ref in enumerate((q_ref, k_ref, v_ref)):
            cols = slice(t * tokd, (t + 1) * tokd)
            x = ref[...]
            ys.append(_conv_silu(x, prev_scr[:, cols], cw_ref[:, cols]))
            prev_scr[:, cols] = x[L - SUBLANES:]
        return ys

    @pl.when(s == 0)
    def _():
        prev_scr[...] = hist_ref[0]

    @pl.when(s == 2 * lag)
    def _():
        s_scr[...] = s0_ref[0]

    if lag:
        @pl.when(s == 0)
        def _():
            s_scr[...] = s0_ref[0]
            for scr in (y_scr, u_scr, w_scr, qe_scr, kd_scr, at_scr, gl_scr):
                scr[...] = jnp.zeros_like(scr)

        front = load()
        ys = [y_scr[t] for t in range(3)]
        _gdn_tail(front, z_ref[...], gn_ref[...], s_scr, o_ref, **kw)
        store(_gdn_mid(ys, gb_ref[...], **kw))
        for t, y in enumerate(prep()):
            y_scr[t] = y
    else:
        _gdn_tail(_gdn_mid(prep(), gb_ref[...], **kw), z_ref[...], gn_ref[...], s_scr, o_ref, **kw)

    @pl.when(last)
    def _():
        sfin_ref[0] = s_scr[...]
        conv_ref[0] = prev_scr[...]


def gdn_mix(proj, gb, hist8, cw8, s0, gnorm, base, *, row0, nseq, T, L, H):
    NC = T // L
    tokd = H * HEAD_DIM
    rb0 = row0 // L
    Hs = GDN_STACK
    n = Hs * L
    ng = H // Hs

    lag = 1 if NC > 1 else 0

    def col(group, behind):
        return lambda b, s: (rb0 + b * NC + jnp.clip(s - behind, 0, NC - 1), group)

    base, base_spec, aliases = _row_range_output(base, proj.shape[0], tokd, 9)
    return pl.pallas_call(
        functools.partial(_gdn_kernel, L=L, Hs=Hs, H=H, lag=lag),
        out_shape=(jax.ShapeDtypeStruct((proj.shape[0], tokd), BF16),
                   jax.ShapeDtypeStruct((nseq, H, HEAD_DIM, HEAD_DIM), F32),
                   jax.ShapeDtypeStruct((nseq, SUBLANES, 3 * tokd), F32)),
        grid=(nseq, NC + 2 * lag),
        in_specs=[pl.BlockSpec((L, tokd), col(0, 0)), pl.BlockSpec((L, tokd), col(1, 0)),
                  pl.BlockSpec((L, tokd), col(2, 0)), pl.BlockSpec((L, tokd), col(3, 2 * lag)),
                  pl.BlockSpec((L, LANES), col(0, lag)),
                  pl.BlockSpec((1, SUBLANES, 3 * tokd), lambda b, c: (b, 0, 0)),
                  pl.BlockSpec((SUBLANES, 3 * tokd), lambda b, c: (0, 0)),
                  pl.BlockSpec((1, H, HEAD_DIM, HEAD_DIM), lambda b, c: (b, 0, 0, 0)),
                  pl.BlockSpec((1, HEAD_DIM), lambda b, c: (0, 0)), base_spec],
        out_specs=(pl.BlockSpec((L, tokd), col(0, 2 * lag)),
                   pl.BlockSpec((1, H, HEAD_DIM, HEAD_DIM), lambda b, c: (b, 0, 0, 0)),
                   pl.BlockSpec((1, SUBLANES, 3 * tokd), lambda b, c: (b, 0, 0))),
        scratch_shapes=[pltpu.VMEM((H, HEAD_DIM, HEAD_DIM), F32),
                        pltpu.VMEM((SUBLANES, 3 * tokd), F32), pltpu.VMEM((3, L, tokd), F32),
                        pltpu.VMEM((ng, n, HEAD_DIM), F32), pltpu.VMEM((ng, n, HEAD_DIM), BF16),
                        pltpu.VMEM((ng, n, HEAD_DIM), BF16), pltpu.VMEM((ng, n, HEAD_DIM), BF16),
                        pltpu.VMEM((ng, n, n), BF16), pltpu.VMEM((H, SUBLANES, LANES), F32)],
        input_output_aliases=aliases,
        compiler_params=_params(("parallel", "arbitrary")),
        name="gdn_mix",
    )(proj, proj, proj, proj, gb, hist8, cw8, s0, gnorm.reshape(1, HEAD_DIM), base)


def _cumsum_kernel(x_ref, o_ref, carry_ref, *, tb):
    @pl.when(pl.program_id(0) == 0)
    def _():
        carry_ref[...] = jnp.zeros_like(carry_ref)

    r = lax.broadcasted_iota(jnp.int32, (tb, tb), 0)
    c = lax.broadcasted_iota(jnp.int32, (tb, tb), 1)
    triu = jnp.where(r <= c, 1.0, 0.0).astype(BF16)
    parts = _split3(x_ref[...])
    out = (_dot(parts[0], triu) + _dot(parts[1], triu) + _dot(parts[2], triu)) + carry_ref[...]
    o_ref[...] = out
    carry_ref[...] = out[:, tb - 1:tb]


def cumsum_lanes(x, *, tb):
    R, T = x.shape
    return pl.pallas_call(
        functools.partial(_cumsum_kernel, tb=tb),
        out_shape=jax.ShapeDtypeStruct((R, T), F32),
        grid=(T // tb,),
        in_specs=[pl.BlockSpec((R, tb), lambda i: (0, i))],
        out_specs=pl.BlockSpec((R, tb), lambda i: (0, i)),
        scratch_shapes=[pltpu.VMEM((R, 1), F32)],
        compiler_params=_params(("arbitrary",)),
        name="cumsum_lanes",
    )(x)


def _fox_prompt_kernel(qi_tab, ki_tab, q_ref, gate_ref, k_ref, v_ref, cq_ref, ck_ref, o_ref,
                       cq_scr, m_scr, acc_scr, *, t, hb, rsub):
    hg = pl.program_id(1)
    pair = pl.program_id(2)
    qi = qi_tab[pair]
    ki = ki_tab[pair]
    tr = t // rsub
    hsl = [slice(j * HEAD_DIM, (j + 1) * HEAD_DIM) for j in range(hb)]

    @pl.when(ki == 0)
    def _():
        lane = lax.broadcasted_iota(jnp.int32, cq_ref.shape, 1)
        for j in range(hb):
            col = jnp.sum(jnp.where(lane == hg * hb + j, cq_ref[...], 0.0), axis=1, keepdims=True) * LOG2E
            cq_scr[j] = jnp.broadcast_to(col, (t, LANES))
        m_scr[...] = jnp.full_like(m_scr, -jnp.inf)
        acc_scr[...] = jnp.zeros_like(acc_scr)

    def step(masked):
        probs = [(j, slice(r * tr, (r + 1) * tr), r * tr, (r + 1) * tr if masked else t)
                 for j in range(hb) for r in range(rsub)]
        ck2 = [ck_ref[0, j:j + 1, :] * LOG2E for j in range(hb)]
        v1 = [jnp.concatenate([v_ref[:, hsl[j]], jnp.ones((t, LANES), BF16)], axis=1) for j in range(hb)]
        ss = [_dot_nt(q_ref[rs, hsl[j]], k_ref[:kc, hsl[j]]) for j, rs, _, kc in probs]
        ss = [s * (SCALE * LOG2E) + (jnp.concatenate([cq_scr[j, rs]] * (kc // LANES), axis=1) - ck2[j][:, :kc])
              for s, (j, rs, _, kc) in zip(ss, probs)]
        if masked:
            ss = [jnp.where(lax.broadcasted_iota(jnp.int32, (tr, kc), 1)
                            <= lax.broadcasted_iota(jnp.int32, (tr, kc), 0) + r0, s, -jnp.inf)
                  for s, (_, _, r0, kc) in zip(ss, probs)]
        m_olds = [m_scr[j, rs] for j, rs, _, _ in probs]
        m_news = [jnp.maximum(mo, jnp.max(s, axis=-1, keepdims=True)) for mo, s in zip(m_olds, ss)]
        ps = [jnp.exp2(s - jnp.concatenate([mn] * (kc // LANES), axis=1)).astype(BF16)
              for s, mn, (_, _, _, kc) in zip(ss, m_news, probs)]
        pvs = [_dot(p, v1[j][:kc]) for p, (j, _, _, kc) in zip(ps, probs)]
        for (j, rs, _, _), mo, mn, pv in zip(probs, m_olds, m_news, pvs):
            alpha = jnp.exp2(mo - mn)
            acc_scr[j, rs] = jnp.concatenate([alpha, alpha], axis=1) * acc_scr[j, rs] + pv
            m_scr[j, rs] = mn

    @pl.when(ki < qi)
    def _():
        step(False)

    @pl.when(ki == qi)
    def _():
        step(True)
        for j in range(hb):
            acc = acc_scr[j]
            o = acc[:, :HEAD_DIM] / acc[:, HEAD_DIM:HEAD_DIM + 1]
            o_ref[:, hsl[j]] = (o * _sigmoid(gate_ref[:, hsl[j]].astype(F32))).astype(o_ref.dtype)


def fox_prompt(proj, karr, varr, c_col, c_rows, *, nseq, T, H, t, hb, rsub):
    nb = T // t
    ng = H // hb
    W = hb * HEAD_DIM
    pairs = [(qi, ki) for qi in range(nb) for ki in range(qi + 1)]
    qi_tab = jnp.asarray([p[0] for p in pairs], jnp.int32)
    ki_tab = jnp.asarray([p[1] for p in pairs], jnp.int32)
    grid_spec = pltpu.PrefetchScalarGridSpec(
        num_scalar_prefetch=2,
        grid=(nseq, ng, len(pairs)),
        in_specs=[pl.BlockSpec((t, W), lambda b, g, p, qt, kt: (b * nb + qt[p], g)),
                  pl.BlockSpec((t, W), lambda b, g, p, qt, kt: (b * nb + qt[p], ng + g)),
                  pl.BlockSpec((t, W), lambda b, g, p, qt, kt: (b * nb + kt[p], g)),
                  pl.BlockSpec((t, W), lambda b, g, p, qt, kt: (b * nb + kt[p], g)),
                  pl.BlockSpec((t, H), lambda b, g, p, qt, kt: (b * nb + qt[p], 0)),
                  pl.BlockSpec((1, hb, t), lambda b, g, p, qt, kt: (b * ng + g, 0, kt[p]))],
        out_specs=pl.BlockSpec((t, W), lambda b, g, p, qt, kt: (b * nb + qt[p], g)),
        scratch_shapes=[pltpu.VMEM((hb, t, LANES), F32), pltpu.VMEM((hb, t, LANES), F32),
                        pltpu.VMEM((hb, t, 2 * HEAD_DIM), F32)])
    return pl.pallas_call(
        functools.partial(_fox_prompt_kernel, t=t, hb=hb, rsub=rsub),
        out_shape=jax.ShapeDtypeStruct((proj.shape[0], H * HEAD_DIM), BF16),
        grid_spec=grid_spec,
        compiler_params=_params(("parallel", "parallel", "arbitrary")),
        name="fox_prompt",
    )(qi_tab, ki_tab, proj, proj, karr, varr, c_col, c_rows.reshape(nseq * ng, hb, T))


def _fox_sample_kernel(q_ref, gate_ref, ck_ref, cv_ref, kn_ref, vn_ref, cq_ref, cc_ref, cn_ref, base_ref, o_ref,
                       m_scr, l_scr, acc_scr, *, H, T):
    del base_ref
    ki = pl.program_id(1)

    @pl.when(ki == 0)
    def _():
        m_scr[...] = jnp.full_like(m_scr, -jnp.inf)
        l_scr[...] = jnp.zeros_like(l_scr)
        acc_scr[...] = jnp.zeros_like(acc_scr)

    cq = cq_ref[...]

    sls = [slice(h * HEAD_DIM, (h + 1) * HEAD_DIM) for h in range(H)]

    def update(k_of, v_of, ck_of, mask):
        ss = [_dot_nt(q_ref[:, sl], k_of(h)) * SCALE + (cq[:, h:h + 1] - ck_of(h)) for h, sl in enumerate(sls)]
        if mask is not None:
            ss = [jnp.where(mask, s, -jnp.inf) for s in ss]
        m_olds = [m_scr[h] for h in range(H)]
        m_news = [jnp.maximum(mo, jnp.max(s, axis=-1, keepdims=True)) for mo, s in zip(m_olds, ss)]
        alphas = [jnp.exp(mo - mn) for mo, mn in zip(m_olds, m_news)]
        ps = [jnp.exp(s - mn) for s, mn in zip(ss, m_news)]
        pvs = [_dot(p.astype(BF16), v_of(h)) for h, p in enumerate(ps)]
        for h, sl in enumerate(sls):
            l_scr[h] = alphas[h] * l_scr[h] + jnp.sum(ps[h], axis=-1, keepdims=True)
            acc_scr[:, sl] = alphas[h] * acc_scr[:, sl] + pvs[h]
            m_scr[h] = m_news[h]

    update(lambda h: ck_ref[0, h].astype(BF16), lambda h: cv_ref[0, h].astype(BF16),
           lambda h: cc_ref[0, h:h + 1, :], None)

    @pl.when(ki == pl.num_programs(1) - 1)
    def _():
        causal = (lax.broadcasted_iota(jnp.int32, (T, T), 1) <= lax.broadcasted_iota(jnp.int32, (T, T), 0))
        update(lambda h: kn_ref[:, sls[h]], lambda h: vn_ref[:, sls[h]], lambda h: cn_ref[0, h:h + 1, :], causal)
        for h, sl in enumerate(sls):
            o = acc_scr[:, sl] / l_scr[h] * _sigmoid(gate_ref[:, sl].astype(F32))
            o_ref[:, sl] = o.astype(o_ref.dtype)


def fox_sample(proj, cache_k, cache_v, k_new, v_new, c_col, c_cache, c_new, base, *, row0, nseq, T, H, tk):
    P = cache_k.shape[2]
    W = H * HEAD_DIM
    rb0 = row0 // T
    cache_spec = pl.BlockSpec((1, H, tk, HEAD_DIM), lambda b, i: (b, 0, i, 0))
    base, base_spec, aliases = _row_range_output(base, proj.shape[0], W, 9)
    new_spec = pl.BlockSpec((T, W), lambda b, i: (b, 0))
    return pl.pallas_call(
        functools.partial(_fox_sample_kernel, H=H, T=T),
        out_shape=jax.ShapeDtypeStruct((proj.shape[0], W), BF16),
        grid=(nseq, P // tk),
        in_specs=[pl.BlockSpec((T, W), lambda b, i: (rb0 + b, 0)),
                  pl.BlockSpec((T, W), lambda b, i: (rb0 + b, 1)),
                  cache_spec, cache_spec, new_spec, new_spec,
                  pl.BlockSpec((T, H), lambda b, i: (b, 0)),
                  pl.BlockSpec((1, H, tk), lambda b, i: (b, 0, i)),
                  pl.BlockSpec((1, H, T), lambda b, i: (b, 0, 0)), base_spec],
        out_specs=pl.BlockSpec((T, W), lambda b, i: (rb0 + b, 0)),
        scratch_shapes=[pltpu.VMEM((H, T, 1), F32), pltpu.VMEM((H, T, 1), F32), pltpu.VMEM((T, W), F32)],
        input_output_aliases=aliases,
        compiler_params=_params(("parallel", "arbitrary")),
        name="fox_sample",
    )(proj, proj, cache_k, cache_v, k_new, v_new, c_col, c_cache, c_new, base)


def _pad_cols(a, width):
    return jnp.pad(a, ((0, 0), (0, width - a.shape[1])))


def kernel(x_prompt, x_sample, state_gdn, state_conv, cache_k, cache_v, cache_logf, cache_mem_k, cache_mem_v, mem_prompt, norm_mix_pre, norm_mix_post, norm_mlp_pre, norm_mlp_post, w_in_a, conv_w_a, a_log, dt_bias, gdn_norm, w_in_b, norm_kv, w_kvf, b_f, norm_mem, w_mem_kv, w_o, w_up, w_down):
    Bp, Tp, D = x_prompt.shape
    Bs, Ts, _ = x_sample.shape
    n_a = w_in_a.shape[0]
    depth = w_o.shape[0]
    H = a_log.shape[1]
    tokd = H * HEAD_DIM
    qkvd = 3 * tokd
    P = cache_k.shape[1]
    nmem = mem_prompt.shape[1]
    mh = cache_mem_k.shape[3]
    memd = mh * HEAD_DIM
    Mp, Ms = Bp * Tp, Bs * Ts
    M = Mp + Ms
    TM = next(t for t in (1024, 512, 256, 128, 64, 32) if M % t == 0)
    TMH = max(TM // 2, 32)
    HR = -(-H // SUBLANES) * SUBLANES

    x = jnp.concatenate([x_prompt.reshape(Mp, D), x_sample.reshape(Ms, D)], axis=0)

    mem_rows = mem_prompt.reshape(Bp * nmem, D)
    wkv = w_mem_kv.astype(BF16)
    pkv = jnp.stack([norm_matmul(mem_rows, norm_mem[l], wkv, layer=l, tm=min(TM, Bp * nmem), tn=memd, out_dtype=F32)
                     for l in range(depth)])
    p_mem_k = pkv[:, :, :memd].reshape(depth, Bp, nmem, mh, HEAD_DIM)
    p_mem_v = pkv[:, :, memd:].reshape(depth, Bp, nmem, mh, HEAD_DIM)
    wo_all = w_o.astype(BF16)
    wup_all = w_up.astype(BF16)
    wdn_all = w_down.astype(BF16)
    wb_all = w_in_b.astype(BF16)

    def hist8(h):
        return jnp.pad(h, ((0, 0), (SUBLANES - (CONV_W - 1), 0), (0, 0)))

    cache_k2 = cache_k.transpose(0, 2, 1, 3)
    cache_v2 = cache_v.transpose(0, 2, 1, 3)

    p_gdn, s_gdn, p_conv, s_conv = [], [], [], []
    lane = jnp.arange(LANES)
    for l in range(depth):
        if l == n_a:
            wkf = jnp.concatenate([w_kvf[:, :tokd], _pad_cols(w_kvf[:, 2 * tokd:], LANES)], axis=1).astype(BF16)
            wv = w_kvf[:, tokd:2 * tokd].astype(BF16)
            pf = jnp.zeros((SUBLANES, LANES), F32).at[0, :H].set(b_f)
            kp2, p_k, lf_p = kv_proj(x, norm_kv, wkf, pf, row0=0, nseq=Bp, T=Tp, tm=TMH, H=H)
            vp2, p_v = kv_proj(x, norm_kv, wv, pf, row0=0, nseq=Bp, T=Tp, tm=TMH, H=H)
            ks2, s_k, lf_s = kv_proj(x, norm_kv, wkf, pf, row0=Mp, nseq=Bs, T=Ts, tm=TMH, H=H)
            vs2, s_v = kv_proj(x, norm_kv, wv, pf, row0=Mp, nseq=Bs, T=Ts, tm=TMH, H=H)
            lf_p = lf_p[:, :H].reshape(Bp, Tp, H)
            lf_s = lf_s[:, :H].reshape(Bs, Ts, H)

            def rows_of(a):
                a = jnp.pad(a.transpose(0, 2, 1), ((0, 0), (0, HR - H), (0, 0)))
                return a.reshape(a.shape[0] * HR, a.shape[2])

            cp_rows = cumsum_lanes(rows_of(lf_p), tb=min(Tp, 512)).reshape(Bp, HR, Tp)[:, :H]
            cp_col = cp_rows.transpose(0, 2, 1).reshape(Mp, H)
            tot = P + Ts
            tot_pad = -(-tot // LANES) * LANES
            lf_all = jnp.pad(jnp.concatenate([cache_logf, lf_s], axis=1), ((0, 0), (0, tot_pad - tot), (0, 0)))
            cs_rows = cumsum_lanes(rows_of(lf_all), tb=LANES).reshape(Bs, HR, tot_pad)[:, :H]
            cs_cache = cs_rows[:, :, :P]
            cs_new = cs_rows[:, :, P:tot]
            cs_col = cs_new.transpose(0, 2, 1).reshape(Ms, H)

        if l < n_a:
            w = w_in_a[l]
            o1 = qkvd + tokd
            w_main = jnp.concatenate([w[:, :o1], w[:, o1 + 2 * H:]], axis=1).astype(BF16)
            w_ab = _pad_cols(w[:, o1:o1 + 2 * H], LANES).astype(BF16)
            pg = jnp.zeros((SUBLANES, LANES), F32)
            pg = pg.at[0, :H].set(a_log[l]).at[1, :H].set(dt_bias[l]).at[2].set((lane < H).astype(F32))
            proj, gb = norm_matmul_side(x, norm_mix_pre[l], w_main, w_ab, pg, _ep_gdn_gates,
                                        tm=TM, tn=_col_tile(w_main.shape[1]))
            cw8 = jnp.pad(conv_w_a[l], ((0, SUBLANES - CONV_W), (0, 0)))
            tok, sp, cp = gdn_mix(proj, gb, hist8(jnp.zeros((Bp, CONV_W - 1, qkvd), F32)), cw8,
                                  jnp.zeros((Bp, H, HEAD_DIM, HEAD_DIM), F32), gdn_norm[l], None,
                                  row0=0, nseq=Bp, T=Tp, L=min(Tp, GDN_CHUNK), H=H)
            tok, ss, cs = gdn_mix(proj, gb, hist8(state_conv[l]), cw8, state_gdn[l], gdn_norm[l], tok,
                                  row0=Mp, nseq=Bs, T=Ts, L=min(Ts, GDN_CHUNK), H=H)
            p_gdn.append(sp)
            s_gdn.append(ss)
            p_conv.append(cp[:, SUBLANES - (CONV_W - 1):])
            s_conv.append(cs[:, SUBLANES - (CONV_W - 1):])
            mq_block = (qkvd + tokd) // memd
        else:
            proj = norm_matmul(x, norm_mix_pre[l], wb_all, layer=l - n_a, tm=TM, tn=_col_tile(wb_all.shape[2]),
                               out_dtype=BF16)
            tfox = min(Tp, 1024)
            tok = fox_prompt(proj, kp2, vp2, cp_col, cp_rows, nseq=Bp, T=Tp, H=H, t=tfox,
                             hb=2, rsub=max(tfox // 256, 1))
            tok = fox_sample(proj, cache_k2, cache_v2, ks2, vs2, cs_col, cs_cache, cs_new, tok,
                             row0=Mp, nseq=Bs, T=Ts, H=H, tk=min(P, 1024))
            mq_block = 2 * tokd // memd
        mem_o = mem_attend(proj, p_mem_k, p_mem_v, l, None, row0=0, rows_per_seq=Tp, tm=min(Tp, 512), sb=1,
                           col_block=mq_block)
        mem_o = mem_attend(proj, cache_mem_k, cache_mem_v, l, mem_o, row0=Mp, rows_per_seq=Ts, tm=Ts,
                           sb=math.gcd(Bs, 8), col_block=mq_block)
        x = out_proj(tok, mem_o, x, wo_all, l, norm_mix_post[l], tm=TMH)
        x = mlp(x, norm_mlp_pre[l], norm_mlp_post[l], wup_all, wdn_all, l, tm=TMH, tf=min(wup_all.shape[2], 1024),
                split_rows=Mp if l == depth - 1 else None)

    y_prompt = x[0].reshape(Bp, Tp, D)
    y_sample = x[1].reshape(Bs, Ts, D)
    return (y_prompt, y_sample, jnp.stack(p_gdn), jnp.stack(p_conv),
            p_k.transpose(0, 2, 1, 3), p_v.transpose(0, 2, 1, 3), lf_p, p_mem_k, p_mem_v,
            jnp.stack(s_gdn), jnp.stack(s_conv),
            s_k.transpose(0, 2, 1, 3), s_v.transpose(0, 2, 1, 3), lf_s)
```

```python
import functools
import math

import jax
import jax.numpy as jnp
from jax import lax
from jax.experimental import pallas as pl
from jax.experimental.pallas import tpu as pltpu

F32 = jnp.float32
BF16 = jnp.bfloat16
EPS = 1e-6
HEAD_DIM = 128
LANES = 128
SUBLANES = 8
SCALE = HEAD_DIM ** -0.5
LOG2E = math.log2(math.e)
CONV_W = 4
GDN_CHUNK = 64
GDN_STACK = 2
SOLVE_BLOCK = 16
V7X_VMEM_BYTES = 64 * 1024 * 1024
VMEM_LIMIT = V7X_VMEM_BYTES * 7 // 8
MAX_COL_TILE = 1792


def _params(sem):
    return pltpu.CompilerParams(dimension_semantics=sem, vmem_limit_bytes=VMEM_LIMIT)


def _rms(x):
    return x * lax.rsqrt(jnp.mean(x * x, axis=-1, keepdims=True) + EPS)


def _sigmoid(x):
    return 1.0 / (1.0 + jnp.exp(-x))


def _softplus(x):
    return jnp.maximum(x, 0.0) + jnp.log1p(jnp.exp(-jnp.abs(x)))


def _dot(a, b):
    return jnp.dot(a, b, preferred_element_type=F32)


def _dot_tn(a, b):
    return lax.dot_general(a, b, (((0,), (0,)), ((), ())), preferred_element_type=F32)


def _dot_nt(a, b):
    return lax.dot_general(a, b, (((1,), (1,)), ((), ())), preferred_element_type=F32)


def _split3(x):
    hi = x.astype(BF16)
    r = x - hi.astype(F32)
    mid = r.astype(BF16)
    return hi, mid, (r - mid.astype(F32)).astype(BF16)


def _ep_gdn_gates(acc, p):
    g = -jnp.exp(p[0:1]) * _softplus(acc + p[1:2])
    return jnp.where(p[2:3] > 0.5, g, _sigmoid(acc))


def _ep_log_forget(acc, p):
    return -_softplus(-(acc + p[0:1]))


def _norm_matmul_kernel(x_ref, g_ref, w_ref, o_ref, h_ref):
    @pl.when(pl.program_id(1) == 0)
    def _():
        h_ref[...] = (_rms(x_ref[...]) * g_ref[...]).astype(BF16)

    o_ref[...] = _dot(h_ref[...], w_ref[...]).astype(o_ref.dtype)


def _norm_matmul_side_kernel(x_ref, g_ref, w_ref, ws_ref, ps_ref, o_ref, os_ref, h_ref, *, side_epilogue):
    @pl.when(pl.program_id(1) == 0)
    def _():
        h = (_rms(x_ref[...]) * g_ref[...]).astype(BF16)
        h_ref[...] = h
        os_ref[...] = side_epilogue(_dot(h, ws_ref[...]), ps_ref[...])

    o_ref[...] = _dot(h_ref[...], w_ref[...]).astype(o_ref.dtype)


def norm_matmul_side(x, g, w, w_side, p_side, side_epilogue, *, tm, tn):
    M, K = x.shape
    N = w.shape[1]
    return pl.pallas_call(
        functools.partial(_norm_matmul_side_kernel, side_epilogue=side_epilogue),
        out_shape=(jax.ShapeDtypeStruct((M, N), F32), jax.ShapeDtypeStruct((M, LANES), F32)),
        grid=(M // tm, N // tn),
        in_specs=[pl.BlockSpec((tm, K), lambda i, j: (i, 0)),
                  pl.BlockSpec((1, K), lambda i, j: (0, 0)),
                  pl.BlockSpec((K, tn), lambda i, j: (0, j)),
                  pl.BlockSpec((K, LANES), lambda i, j: (0, 0)),
                  pl.BlockSpec((SUBLANES, LANES), lambda i, j: (0, 0))],
        out_specs=(pl.BlockSpec((tm, tn), lambda i, j: (i, j)),
                   pl.BlockSpec((tm, LANES), lambda i, j: (i, 0))),
        scratch_shapes=[pltpu.VMEM((tm, K), BF16)],
        compiler_params=_params(("parallel", "arbitrary")),
        name="norm_matmul_side",
    )(x, g.reshape(1, K), w, w_side, p_side)


def _col_tile(n):
    return max(c for c in range(LANES, min(n, MAX_COL_TILE) + 1, LANES) if n % c == 0)


def norm_matmul(x, g, w, *, tm, tn, layer, out_dtype):
    M, K = x.shape
    N = w.shape[2]
    return pl.pallas_call(
        _norm_matmul_kernel,
        out_shape=jax.ShapeDtypeStruct((M, N), out_dtype),
        grid=(M // tm, N // tn),
        in_specs=[pl.BlockSpec((tm, K), lambda i, j: (i, 0)),
                  pl.BlockSpec((1, K), lambda i, j: (0, 0)),
                  pl.BlockSpec((None, K, tn), lambda i, j: (layer, 0, j))],
        out_specs=pl.BlockSpec((tm, tn), lambda i, j: (i, j)),
        scratch_shapes=[pltpu.VMEM((tm, K), BF16)],
        compiler_params=_params(("parallel", "arbitrary")),
        name="norm_matmul",
    )(x, g.reshape(1, K), w)


def _kv_proj_kernel(x_ref, g_ref, w_ref, pf_ref, o2_ref, o4_ref, *lf_ref, H, ns, tt):
    h = (_rms(x_ref[...]) * g_ref[...]).astype(BF16)
    acc = _dot(h, w_ref[...])
    tokd = H * HEAD_DIM
    o2_ref[...] = acc[:, :tokd].astype(o2_ref.dtype)
    for s in range(ns):
        for hh in range(H):
            o4_ref[s, hh] = acc[s * tt:(s + 1) * tt, hh * HEAD_DIM:(hh + 1) * HEAD_DIM]
    if lf_ref:
        lf_ref[0][...] = _ep_log_forget(acc[:, tokd:], pf_ref[...])


def kv_proj(x, g, w, pf, *, row0, nseq, T, tm, H):
    K = x.shape[1]
    N = w.shape[1]
    tokd = H * HEAD_DIM
    with_logf = N == tokd + LANES
    assert with_logf or N == tokd
    rows = nseq * T
    rb0 = row0 // tm
    ns, tt = (tm // T, T) if tm >= T else (1, tm)
    parts = T // tt
    return pl.pallas_call(
        functools.partial(_kv_proj_kernel, H=H, ns=ns, tt=tt),
        out_shape=(jax.ShapeDtypeStruct((rows, tokd), BF16),
                   jax.ShapeDtypeStruct((nseq, H, T, HEAD_DIM), F32))
        + ((jax.ShapeDtypeStruct((rows, LANES), F32),) if with_logf else ()),
        grid=(rows // tm,),
        in_specs=[pl.BlockSpec((tm, K), lambda i: (rb0 + i, 0)),
                  pl.BlockSpec((1, K), lambda i: (0, 0)),
                  pl.BlockSpec((K, N), lambda i: (0, 0)),
                  pl.BlockSpec((SUBLANES, LANES), lambda i: (0, 0))],
        out_specs=(pl.BlockSpec((tm, tokd), lambda i: (i, 0)),
                   pl.BlockSpec((ns, H, tt, HEAD_DIM), lambda i: (i // parts, 0, i % parts, 0)))
        + ((pl.BlockSpec((tm, LANES), lambda i: (i, 0)),) if with_logf else ()),
        compiler_params=_params(("parallel",)),
        name="kv_proj",
    )(x, g.reshape(1, K), w, pf)


def _out_proj_kernel(tok_ref, mem_ref, x_ref, wa_ref, wb_ref, g_ref, o_ref):
    mix = _dot(tok_ref[...], wa_ref[...]) + _dot(mem_ref[...], wb_ref[...])
    o_ref[...] = x_ref[...] + _rms(mix) * g_ref[...]


def out_proj(tok, mem, x, w, layer, g, *, tm):
    M, D = x.shape
    Ka, Kb = tok.shape[1], mem.shape[1]
    assert Ka % Kb == 0
    return pl.pallas_call(
        _out_proj_kernel,
        out_shape=jax.ShapeDtypeStruct((M, D), F32),
        grid=(M // tm,),
        in_specs=[pl.BlockSpec((tm, Ka), lambda i: (i, 0)),
                  pl.BlockSpec((tm, Kb), lambda i: (i, 0)),
                  pl.BlockSpec((tm, D), lambda i: (i, 0)),
                  pl.BlockSpec((None, Ka, D), lambda i: (layer, 0, 0)),
                  pl.BlockSpec((None, Kb, D), lambda i: (layer, Ka // Kb, 0)),
                  pl.BlockSpec((1, D), lambda i: (0, 0))],
        out_specs=pl.BlockSpec((tm, D), lambda i: (i, 0)),
        compiler_params=_params(("parallel",)),
        name="out_proj",
    )(tok, mem, x, w, w, g.reshape(1, D))


def _mlp_kernel(x_ref, gpre_ref, gpost_ref, wup_ref, wdn_ref, *rest, split):
    outs, (h_ref, acc_ref) = rest[:-2], rest[-2:]
    j = pl.program_id(1)

    @pl.when(j == 0)
    def _():
        h_ref[...] = (_rms(x_ref[...]) * gpre_ref[...]).astype(BF16)
        acc_ref[...] = jnp.zeros_like(acc_ref)

    up = _dot(h_ref[...], wup_ref[...])
    act = jnp.square(jnp.maximum(up, 0.0)).astype(BF16)
    acc_ref[...] += _dot(act, wdn_ref[...])

    def result():
        return x_ref[...] + _rms(acc_ref[...]) * gpost_ref[...]

    last = j == pl.num_programs(1) - 1
    if split is None:
        @pl.when(last)
        def _():
            outs[0][...] = result()
    else:
        first_group = pl.program_id(0) < split

        @pl.when(last & first_group)
        def _():
            outs[0][...] = result()

        @pl.when(last & jnp.logical_not(first_group))
        def _():
            outs[1][...] = result()


def mlp(x, gpre, gpost, wup, wdn, layer, *, tm, tf, split_rows=None):
    M, D = x.shape
    FF = wup.shape[2]
    if split_rows is None:
        split = None
        out_shape = jax.ShapeDtypeStruct((M, D), F32)
        out_specs = pl.BlockSpec((tm, D), lambda i, j: (i, 0))
    else:
        split = split_rows // tm
        assert split_rows % tm == 0 and 0 < split < M // tm
        out_shape = (jax.ShapeDtypeStruct((split_rows, D), F32), jax.ShapeDtypeStruct((M - split_rows, D), F32))
        out_specs = (pl.BlockSpec((tm, D), lambda i, j: (jnp.minimum(i, split - 1), 0)),
                     pl.BlockSpec((tm, D), lambda i, j: (jnp.maximum(i - split, 0), 0)))
    return pl.pallas_call(
        functools.partial(_mlp_kernel, split=split),
        out_shape=out_shape,
        grid=(M // tm, FF // tf),
        in_specs=[pl.BlockSpec((tm, D), lambda i, j: (i, 0)),
                  pl.BlockSpec((1, D), lambda i, j: (0, 0)),
                  pl.BlockSpec((1, D), lambda i, j: (0, 0)),
                  pl.BlockSpec((None, D, tf), lambda i, j: (layer, 0, j)),
                  pl.BlockSpec((None, tf, D), lambda i, j: (layer, j, 0))],
        out_specs=out_specs,
        scratch_shapes=[pltpu.VMEM((tm, D), BF16), pltpu.VMEM((tm, D), F32)],
        compiler_params=_params(("arbitrary", "arbitrary")),
        name="mlp",
    )(x, gpre.reshape(1, D), gpost.reshape(1, D), wup, wdn)


def _row_range_output(base, total_rows, width, n_inputs):
    spec = pl.BlockSpec(memory_space=pl.ANY)
    if base is None:
        return jnp.zeros((SUBLANES, LANES), BF16), spec, {}
    assert base.shape == (total_rows, width) and base.dtype == BF16
    return base, spec, {n_inputs: 0}


def _mem_attn_kernel(q_ref, mk_ref, mv_ref, base_ref, o_ref, *, heads, sb, tm):
    del base_ref
    probs = [(s, slice(s * tm, (s + 1) * tm), h, slice(h * HEAD_DIM, (h + 1) * HEAD_DIM))
             for s in range(sb) for h in range(heads)]
    ss = [_dot_nt(q_ref[rs, sl].astype(BF16), mk_ref[0, s, :, h, :].astype(BF16)) * SCALE
          for s, rs, h, sl in probs]
    es = [jnp.exp(s - jnp.max(s, axis=-1, keepdims=True)) for s in ss]
    ps = [(e / jnp.sum(e, axis=-1, keepdims=True)).astype(BF16) for e in es]
    outs = [_dot(p, mv_ref[0, s, :, h, :].astype(BF16)) for p, (s, _, h, _) in zip(ps, probs)]
    for o, (_, rs, _, sl) in zip(outs, probs):
        o_ref[rs, sl] = o.astype(o_ref.dtype)


def mem_attend(proj, mk, mv, layer, base, *, row0, rows_per_seq, tm, sb, col_block):
    _, nseq, nmem, heads, _ = mk.shape
    assert sb == 1 or tm == rows_per_seq
    W = heads * HEAD_DIM
    nt = rows_per_seq // tm
    rb0 = row0 // (sb * tm)
    mem_spec = pl.BlockSpec((1, sb, nmem, heads, HEAD_DIM), lambda b, i: (layer, b, 0, 0, 0))
    base, base_spec, aliases = _row_range_output(base, proj.shape[0], W, 3)
    return pl.pallas_call(
        functools.partial(_mem_attn_kernel, heads=heads, sb=sb, tm=tm),
        out_shape=jax.ShapeDtypeStruct((proj.shape[0], W), BF16),
        grid=(nseq // sb, nt),
        in_specs=[pl.BlockSpec((sb * tm, W), lambda b, i: (rb0 + b * nt + i, col_block)), mem_spec, mem_spec,
                  base_spec],
        out_specs=pl.BlockSpec((sb * tm, W), lambda b, i: (rb0 + b * nt + i, 0)),
        input_output_aliases=aliases,
        compiler_params=_params(("parallel", "parallel")),
        name="mem_attend",
    )(proj, mk, mv, base)


def _split2(x):
    hi = x.astype(BF16)
    return hi, (x - hi.astype(F32)).astype(BF16)


def _dot3(a, b):
    return _dot(a[0], b[0]) + (_dot(a[0], b[1]) + _dot(a[1], b[0]))


def _unit_lower_solve(As, rhss, n, L):
    row = lax.broadcasted_iota(jnp.int32, (n, n), 0)
    col = lax.broadcasted_iota(jnp.int32, (n, n), 1)
    shift = SOLVE_BLOCK.bit_length() - 1
    same = (row >> shift) == (col >> shift)
    es = [jnp.where(same, -A, 0.0) for A in As]
    offs = [jnp.where(same, 0.0, A) for A in As]
    pss = [_split2(e) for e in es]
    ess = pss
    span = 2
    while span < SOLVE_BLOCK:
        ps = [_dot3(p, p) for p in pss]
        pss = [_split2(p) for p in ps]
        es = [e + p + _dot3(e2, p2) for e, p, e2, p2 in zip(es, ps, ess, pss)]
        ess = [_split2(e) for e in es]
        span *= 2
    ebs = [e2[0] for e2 in ess]

    def bdot(a, b):
        return _dot(a.astype(BF16), b.astype(BF16))

    ys = [r + bdot(eb, r) for eb, r in zip(ebs, rhss)]
    nblocks = L // SOLVE_BLOCK
    if nblocks > 1:
        powers = [[o + bdot(eb, o) for eb, o in zip(ebs, offs)]]
        span = 2
        while span < nblocks:
            powers.append([bdot(p, p) for p in powers[-1]])
            span *= 2
        for pw in reversed(powers[1:]):
            ys = [y + bdot(p, y) for p, y in zip(pw, ys)]
        ys = [y - bdot(p, y) for p, y in zip(powers[0], ys)]
    return ys


def _stack_heads(t, heads):
    return jnp.concatenate([t[:, h * HEAD_DIM:(h + 1) * HEAD_DIM] for h in heads], axis=0)


def _conv_silu(x, prev, w):
    row8 = lax.broadcasted_iota(jnp.int32, (SUBLANES, 1), 0)
    y = None
    for s in range(CONV_W - 1, 0, -1):
        sh = pltpu.roll(x, s, 0)
        top = jnp.where(row8 >= s, sh[:SUBLANES], pltpu.roll(prev, s, 0))
        term = jnp.concatenate([top, sh[SUBLANES:]], axis=0) * w[CONV_W - 1 - s:CONV_W - s]
        y = term if y is None else y + term
    y = y + x * w[CONV_W - 1:CONV_W]
    return y * _sigmoid(y)


def _gdn_mid(ys, gb, *, L, Hs, H):
    n = Hs * L
    lshift = L.bit_length() - 1
    yq, yk, yv = ys

    r2 = lax.broadcasted_iota(jnp.int32, (n, n), 0)
    c2 = lax.broadcasted_iota(jnp.int32, (n, n), 1)
    same = (r2 >> lshift) == (c2 >> lshift)
    incl = same & (r2 >= c2)
    strict = same & (r2 > c2)
    stack = _stack_heads
    groups = [[gi * Hs + j for j in range(Hs)] for gi in range(H // Hs)]

    rl = lax.broadcasted_iota(jnp.int32, (L, L), 0)
    cl = lax.broadcasted_iota(jnp.int32, (L, L), 1)
    tril = jnp.where(rl >= cl, 1.0, 0.0).astype(BF16)
    gcum = sum(_dot(tril, part) for part in _split3(gb))
    gcum_t = jnp.concatenate([gcum, jnp.zeros((LANES - L, LANES), F32)], axis=0).T if L < LANES else gcum.T

    def col(tile, heads, off=0):
        return jnp.concatenate([tile[:, off + h:off + h + 1] for h in heads], axis=0)

    Gs = [col(gcum, hs) for hs in groups]
    Grs = [jnp.concatenate([gcum_t[h:h + 1, :L] for h in hs], axis=1) for hs in groups]
    betas = [col(gb, hs, H) for hs in groups]
    glasts = [[gcum[L - 1:L, h:h + 1] for h in hs] for hs in groups]
    GLs = [jnp.concatenate([jnp.broadcast_to(g, (L, 1)) for g in gl], axis=0) for gl in glasts]

    def l2n(t):
        return t * lax.rsqrt(jnp.sum(t * t, axis=-1, keepdims=True) + EPS)

    qs = [l2n(stack(yq, hs)) * SCALE for hs in groups]
    ks = [l2n(stack(yk, hs)) for hs in groups]
    vs = [stack(yv, hs) for hs in groups]
    eGs = [jnp.exp(G) for G in Gs]
    decays = [jnp.exp(jnp.where(incl, G - Gr, -jnp.inf)) for G, Gr in zip(Gs, Grs)]
    kbs = [k.astype(BF16) for k in ks]
    kks = [_dot_nt(kb, kb) for kb in kbs]
    qks = [_dot_nt(q.astype(BF16), kb) for q, kb in zip(qs, kbs)]
    As = [jnp.where(strict, b * kk * d, 0.0) for b, kk, d in zip(betas, kks, decays)]
    rhss = [jnp.concatenate([b * v, (b * eG) * k], axis=-1) for b, v, eG, k in zip(betas, vs, eGs, ks)]
    sols = _unit_lower_solve(As, rhss, n, L)
    us = [s[:, :HEAD_DIM] for s in sols]
    ws = [s[:, HEAD_DIM:].astype(BF16) for s in sols]
    qes = [(q * eG).astype(BF16) for q, eG in zip(qs, eGs)]
    kds = [(k * jnp.exp(GL - G)).astype(BF16) for k, GL, G in zip(ks, GLs, Gs)]
    attns = [(qk * d).astype(BF16) for qk, d in zip(qks, decays)]
    return us, ws, qes, kds, attns, glasts


def _gdn_tail(front, z, gnorm, s_scr, o_ref, *, L, Hs, H):
    us, ws, qes, kds, attns, glasts = front
    groups = [[gi * Hs + j for j in range(Hs)] for gi in range(H // Hs)]
    rsl = [slice(j * L, (j + 1) * L) for j in range(Hs)]
    sbs = [[s_scr[h].astype(BF16) for h in hs] for hs in groups]
    v_news = [jnp.concatenate([u[rs] - _dot(w[rs], sb) for rs, sb in zip(rsl, sbg)], axis=0).astype(BF16)
              for u, w, sbg in zip(us, ws, sbs)]
    outs = [jnp.concatenate([_dot(qe[rs], sb) for rs, sb in zip(rsl, sbg)], axis=0) + _dot(at, vn)
            for qe, sbg, at, vn in zip(qes, sbs, attns, v_news)]
    for hs, gl, kd, vn in zip(groups, glasts, kds, v_news):
        for j, h in enumerate(hs):
            s_scr[h] = jnp.exp(gl[j]) * s_scr[h] + _dot_tn(kd[rsl[j]], vn[rsl[j]])
    for hs, out in zip(groups, outs):
        zz = _stack_heads(z, hs)
        o = (_rms(out) * gnorm * (zz * _sigmoid(zz))).astype(o_ref.dtype)
        for j, h in enumerate(hs):
            o_ref[:, h * HEAD_DIM:(h + 1) * HEAD_DIM] = o[rsl[j]]


def _gdn_kernel(q_ref, k_ref, v_ref, z_ref, gb_ref, hist_ref, cw_ref, s0_ref, gn_ref, base_ref,
                o_ref, sfin_ref, conv_ref,
                s_scr, prev_scr, y_scr, u_scr, w_scr, qe_scr, kd_scr, at_scr, gl_scr, *, L, Hs, H, lag):
    del base_ref
    s = pl.program_id(1)
    last = s == pl.num_programs(1) - 1
    tokd = H * HEAD_DIM
    kw = dict(L=L, Hs=Hs, H=H)
    ngroups = H // Hs

    def store(front):
        us, ws, qes, kds, attns, glasts = front
        for g in range(ngroups):
            u_scr[g], w_scr[g], qe_scr[g], kd_scr[g], at_scr[g] = us[g], ws[g], qes[g], kds[g], attns[g]
            for j, gl in enumerate(glasts[g]):
                gl_scr[g * Hs + j] = jnp.broadcast_to(gl, (SUBLANES, LANES))

    def load():
        rng = range(ngroups)
        glasts = [[gl_scr[g * Hs + j][0:1, 0:1] for j in range(Hs)] for g in rng]
        return ([u_scr[g] for g in rng], [w_scr[g] for g in rng], [qe_scr[g] for g in rng],
                [kd_scr[g] for g in rng], [at_scr[g] for g in rng], glasts)

    def prep():
        ys = []
        for t, ref in enumerate((q_ref, k_ref, v_ref)):
            cols = slice(t * tokd, (t + 1) * tokd)
            x = ref[...]
            ys.append(_conv_silu(x, prev_scr[:, cols], cw_ref[:, cols]))
            prev_scr[:, cols] = x[L - SUBLANES:]
        return ys

    @pl.when(s == 0)
    def _():
        prev_scr[...] = hist_ref[0]

    @pl.when(s == 2 * lag)
    def _():
        s_scr[...] = s0_ref[0]

    if lag:
        @pl.when(s == 0)
        def _():
            s_scr[...] = s0_ref[0]
            for scr in (y_scr, u_scr, w_scr, qe_scr, kd_scr, at_scr, gl_scr):
                scr[...] = jnp.zeros_like(scr)

        front = load()
        ys = [y_scr[t] for t in range(3)]
        _gdn_tail(front, z_ref[...], gn_ref[...], s_scr, o_ref, **kw)
        store(_gdn_mid(ys, gb_ref[...], **kw))
        for t, y in enumerate(prep()):
            y_scr[t] = y
    else:
        _gdn_tail(_gdn_mid(prep(), gb_ref[...], **kw), z_ref[...], gn_ref[...], s_scr, o_ref, **kw)

    @pl.when(last)
    def _():
        sfin_ref[0] = s_scr[...]
        conv_ref[0] = prev_scr[...]


def gdn_mix(proj, gb, hist8, cw8, s0, gnorm, base, *, row0, nseq, T, L, H):
    NC = T // L
    tokd = H * HEAD_DIM
    rb0 = row0 // L
    Hs = GDN_STACK
    n = Hs * L
    ng = H // Hs

    lag = 1 if NC > 1 else 0

    def col(group, behind):
        return lambda b, s: (rb0 + b * NC + jnp.clip(s - behind, 0, NC - 1), group)

    base, base_spec, aliases = _row_range_output(base, proj.shape[0], tokd, 9)
    return pl.pallas_call(
        functools.partial(_gdn_kernel, L=L, Hs=Hs, H=H, lag=lag),
        out_shape=(jax.ShapeDtypeStruct((proj.shape[0], tokd), BF16),
                   jax.ShapeDtypeStruct((nseq, H, HEAD_DIM, HEAD_DIM), F32),
                   jax.ShapeDtypeStruct((nseq, SUBLANES, 3 * tokd), F32)),
        grid=(nseq, NC + 2 * lag),
        in_specs=[pl.BlockSpec((L, tokd), col(0, 0)), pl.BlockSpec((L, tokd), col(1, 0)),
                  pl.BlockSpec((L, tokd), col(2, 0)), pl.BlockSpec((L, tokd), col(3, 2 * lag)),
                  pl.BlockSpec((L, LANES), col(0, lag)),
                  pl.BlockSpec((1, SUBLANES, 3 * tokd), lambda b, c: (b, 0, 0)),
                  pl.BlockSpec((SUBLANES, 3 * tokd), lambda b, c: (0, 0)),
                  pl.BlockSpec((1, H, HEAD_DIM, HEAD_DIM), lambda b, c: (b, 0, 0, 0)),
                  pl.BlockSpec((1, HEAD_DIM), lambda b, c: (0, 0)), base_spec],
        out_specs=(pl.BlockSpec((L, tokd), col(0, 2 * lag)),
                   pl.BlockSpec((1, H, HEAD_DIM, HEAD_DIM), lambda b, c: (b, 0, 0, 0)),
                   pl.BlockSpec((1, SUBLANES, 3 * tokd), lambda b, c: (b, 0, 0))),
        scratch_shapes=[pltpu.VMEM((H, HEAD_DIM, HEAD_DIM), F32),
                        pltpu.VMEM((SUBLANES, 3 * tokd), F32), pltpu.VMEM((3, L, tokd), F32),
                        pltpu.VMEM((ng, n, HEAD_DIM), F32), pltpu.VMEM((ng, n, HEAD_DIM), BF16),
                        pltpu.VMEM((ng, n, HEAD_DIM), BF16), pltpu.VMEM((ng, n, HEAD_DIM), BF16),
                        pltpu.VMEM((ng, n, n), BF16), pltpu.VMEM((H, SUBLANES, LANES), F32)],
        input_output_aliases=aliases,
        compiler_params=_params(("parallel", "arbitrary")),
        name="gdn_mix",
    )(proj, proj, proj, proj, gb, hist8, cw8, s0, gnorm.reshape(1, HEAD_DIM), base)


def _cumsum_kernel(x_ref, o_ref, carry_ref, *, tb):
    @pl.when(pl.program_id(0) == 0)
    def _():
        carry_ref[...] = jnp.zeros_like(carry_ref)

    r = lax.broadcasted_iota(jnp.int32, (tb, tb), 0)
    c = lax.broadcasted_iota(jnp.int32, (tb, tb), 1)
    triu = jnp.where(r <= c, 1.0, 0.0).astype(BF16)
    parts = _split3(x_ref[...])
    out = (_dot(parts[0], triu) + _dot(parts[1], triu) + _dot(parts[2], triu)) + carry_ref[...]
    o_ref[...] = out
    carry_ref[...] = out[:, tb - 1:tb]


def cumsum_lanes(x, *, tb):
    R, T = x.shape
    return pl.pallas_call(
        functools.partial(_cumsum_kernel, tb=tb),
        out_shape=jax.ShapeDtypeStruct((R, T), F32),
        grid=(T // tb,),
        in_specs=[pl.BlockSpec((R, tb), lambda i: (0, i))],
        out_specs=pl.BlockSpec((R, tb), lambda i: (0, i)),
        scratch_shapes=[pltpu.VMEM((R, 1), F32)],
        compiler_params=_params(("arbitrary",)),
        name="cumsum_lanes",
    )(x)


def _fox_prompt_kernel(qi_tab, ki_tab, q_ref, gate_ref, k_ref, v_ref, cq_ref, ck_ref, o_ref,
                       cq_scr, m_scr, acc_scr, *, t, hb, rsub):
    hg = pl.program_id(1)
    pair = pl.program_id(2)
    qi = qi_tab[pair]
    ki = ki_tab[pair]
    tr = t // rsub
    hsl = [slice(j * HEAD_DIM, (j + 1) * HEAD_DIM) for j in range(hb)]

    @pl.when(ki == 0)
    def _():
        lane = lax.broadcasted_iota(jnp.int32, cq_ref.shape, 1)
        for j in range(hb):
            col = jnp.sum(jnp.where(lane == hg * hb + j, cq_ref[...], 0.0), axis=1, keepdims=True) * LOG2E
            cq_scr[j] = jnp.broadcast_to(col, (t, LANES))
        m_scr[...] = jnp.full_like(m_scr, -jnp.inf)
        acc_scr[...] = jnp.zeros_like(acc_scr)

    def step(masked):
        probs = [(j, slice(r * tr, (r + 1) * tr), r * tr, (r + 1) * tr if masked else t)
                 for j in range(hb) for r in range(rsub)]
        ck2 = [ck_ref[0, j:j + 1, :] * LOG2E for j in range(hb)]
        v1 = [jnp.concatenate([v_ref[:, hsl[j]], jnp.ones((t, LANES), BF16)], axis=1) for j in range(hb)]
        ss = [_dot_nt(q_ref[rs, hsl[j]], k_ref[:kc, hsl[j]]) for j, rs, _, kc in probs]
        ss = [s * (SCALE * LOG2E) + (jnp.concatenate([cq_scr[j, rs]] * (kc // LANES), axis=1) - ck2[j][:, :kc])
              for s, (j, rs, _, kc) in zip(ss, probs)]
        if masked:
            ss = [jnp.where(lax.broadcasted_iota(jnp.int32, (tr, kc), 1)
                            <= lax.broadcasted_iota(jnp.int32, (tr, kc), 0) + r0, s, -jnp.inf)
                  for s, (_, _, r0, kc) in zip(ss, probs)]
        m_olds = [m_scr[j, rs] for j, rs, _, _ in probs]
        m_news = [jnp.maximum(mo, jnp.max(s, axis=-1, keepdims=True)) for mo, s in zip(m_olds, ss)]
        ps = [jnp.exp2(s - jnp.concatenate([mn] * (kc // LANES), axis=1)).astype(BF16)
              for s, mn, (_, _, _, kc) in zip(ss, m_news, probs)]
        pvs = [_dot(p, v1[j][:kc]) for p, (j, _, _, kc) in zip(ps, probs)]
        for (j, rs, _, _), mo, mn, pv in zip(probs, m_olds, m_news, pvs):
            alpha = jnp.exp2(mo - mn)
            acc_scr[j, rs] = jnp.concatenate([alpha, alpha], axis=1) * acc_scr[j, rs] + pv
            m_scr[j, rs] = mn

    @pl.when(ki < qi)
    def _():
        step(False)

    @pl.when(ki == qi)
    def _():
        step(True)
        for j in range(hb):
            acc = acc_scr[j]
            o = acc[:, :HEAD_DIM] / acc[:, HEAD_DIM:HEAD_DIM + 1]
            o_ref[:, hsl[j]] = (o * _sigmoid(gate_ref[:, hsl[j]].astype(F32))).astype(o_ref.dtype)


def fox_prompt(proj, karr, varr, c_col, c_rows, *, nseq, T, H, t, hb, rsub):
    nb = T // t
    ng = H // hb
    W = hb * HEAD_DIM
    pairs = [(qi, ki) for qi in range(nb) for ki in range(qi + 1)]
    qi_tab = jnp.asarray([p[0] for p in pairs], jnp.int32)
    ki_tab = jnp.asarray([p[1] for p in pairs], jnp.int32)
    grid_spec = pltpu.PrefetchScalarGridSpec(
        num_scalar_prefetch=2,
        grid=(nseq, ng, len(pairs)),
        in_specs=[pl.BlockSpec((t, W), lambda b, g, p, qt, kt: (b * nb + qt[p], g)),
                  pl.BlockSpec((t, W), lambda b, g, p, qt, kt: (b * nb + qt[p], ng + g)),
                  pl.BlockSpec((t, W), lambda b, g, p, qt, kt: (b * nb + kt[p], g)),
                  pl.BlockSpec((t, W), lambda b, g, p, qt, kt: (b * nb + kt[p], g)),
                  pl.BlockSpec((t, H), lambda b, g, p, qt, kt: (b * nb + qt[p], 0)),
                  pl.BlockSpec((1, hb, t), lambda b, g, p, qt, kt: (b * ng + g, 0, kt[p]))],
        out_specs=pl.BlockSpec((t, W), lambda b, g, p, qt, kt: (b * nb + qt[p], g)),
        scratch_shapes=[pltpu.VMEM((hb, t, LANES), F32), pltpu.VMEM((hb, t, LANES), F32),
                        pltpu.VMEM((hb, t, 2 * HEAD_DIM), F32)])
    return pl.pallas_call(
        functools.partial(_fox_prompt_kernel, t=t, hb=hb, rsub=rsub),
        out_shape=jax.ShapeDtypeStruct((proj.shape[0], H * HEAD_DIM), BF16),
        grid_spec=grid_spec,
        compiler_params=_params(("parallel", "parallel", "arbitrary")),
        name="fox_prompt",
    )(qi_tab, ki_tab, proj, proj, karr, varr, c_col, c_rows.reshape(nseq * ng, hb, T))


def _fox_sample_kernel(q_ref, gate_ref, ck_ref, cv_ref, kn_ref, vn_ref, cq_ref, cc_ref, cn_ref, base_ref, o_ref,
                       m_scr, l_scr, acc_scr, *, H, T):
    del base_ref
    ki = pl.program_id(1)

    @pl.when(ki == 0)
    def _():
        m_scr[...] = jnp.full_like(m_scr, -jnp.inf)
        l_scr[...] = jnp.zeros_like(l_scr)
        acc_scr[...] = jnp.zeros_like(acc_scr)

    cq = cq_ref[...]

    sls = [slice(h * HEAD_DIM, (h + 1) * HEAD_DIM) for h in range(H)]

    def update(k_of, v_of, ck_of, mask):
        ss = [_dot_nt(q_ref[:, sl], k_of(h)) * SCALE + (cq[:, h:h + 1] - ck_of(h)) for h, sl in enumerate(sls)]
        if mask is not None:
            ss = [jnp.where(mask, s, -jnp.inf) for s in ss]
        m_olds = [m_scr[h] for h in range(H)]
        m_news = [jnp.maximum(mo, jnp.max(s, axis=-1, keepdims=True)) for mo, s in zip(m_olds, ss)]
        alphas = [jnp.exp(mo - mn) for mo, mn in zip(m_olds, m_news)]
        ps = [jnp.exp(s - mn) for s, mn in zip(ss, m_news)]
        pvs = [_dot(p.astype(BF16), v_of(h)) for h, p in enumerate(ps)]
        for h, sl in enumerate(sls):
            l_scr[h] = alphas[h] * l_scr[h] + jnp.sum(ps[h], axis=-1, keepdims=True)
            acc_scr[:, sl] = alphas[h] * acc_scr[:, sl] + pvs[h]
            m_scr[h] = m_news[h]

    update(lambda h: ck_ref[0, h].astype(BF16), lambda h: cv_ref[0, h].astype(BF16),
           lambda h: cc_ref[0, h:h + 1, :], None)

    @pl.when(ki == pl.num_programs(1) - 1)
    def _():
        causal = (lax.broadcasted_iota(jnp.int32, (T, T), 1) <= lax.broadcasted_iota(jnp.int32, (T, T), 0))
        update(lambda h: kn_ref[:, sls[h]], lambda h: vn_ref[:, sls[h]], lambda h: cn_ref[0, h:h + 1, :], causal)
        for h, sl in enumerate(sls):
            o = acc_scr[:, sl] / l_scr[h] * _sigmoid(gate_ref[:, sl].astype(F32))
            o_ref[:, sl] = o.astype(o_ref.dtype)


def fox_sample(proj, cache_k, cache_v, k_new, v_new, c_col, c_cache, c_new, base, *, row0, nseq, T, H, tk):
    P = cache_k.shape[2]
    W = H * HEAD_DIM
    rb0 = row0 // T
    cache_spec = pl.BlockSpec((1, H, tk, HEAD_DIM), lambda b, i: (b, 0, i, 0))
    base, base_spec, aliases = _row_range_output(base, proj.shape[0], W, 9)
    new_spec = pl.BlockSpec((T, W), lambda b, i: (b, 0))
    return pl.pallas_call(
        functools.partial(_fox_sample_kernel, H=H, T=T),
        out_shape=jax.ShapeDtypeStruct((proj.shape[0], W), BF16),
        grid=(nseq, P // tk),
        in_specs=[pl.BlockSpec((T, W), lambda b, i: (rb0 + b, 0)),
                  pl.BlockSpec((T, W), lambda b, i: (rb0 + b, 1)),
                  cache_spec, cache_spec, new_spec, new_spec,
                  pl.BlockSpec((T, H), lambda b, i: (b, 0)),
                  pl.BlockSpec((1, H, tk), lambda b, i: (b, 0, i)),
                  pl.BlockSpec((1, H, T), lambda b, i: (b, 0, 0)), base_spec],
        out_specs=pl.BlockSpec((T, W), lambda b, i: (rb0 + b, 0)),
        scratch_shapes=[pltpu.VMEM((H, T, 1), F32), pltpu.VMEM((H, T, 1), F32), pltpu.VMEM((T, W), F32)],
        input_output_aliases=aliases,
        compiler_params=_params(("parallel", "arbitrary")),
        name="fox_sample",
    )(proj, proj, cache_k, cache_v, k_new, v_new, c_col, c_cache, c_new, base)


def _pad_cols(a, width):
    return jnp.pad(a, ((0, 0), (0, width - a.shape[1])))


def kernel(x_prompt, x_sample, state_gdn, state_conv, cache_k, cache_v, cache_logf, cache_mem_k, cache_mem_v, mem_prompt, norm_mix_pre, norm_mix_post, norm_mlp_pre, norm_mlp_post, w_in_a, conv_w_a, a_log, dt_bias, gdn_norm, w_in_b, norm_kv, w_kvf, b_f, norm_mem, w_mem_kv, w_o, w_up, w_down):
    Bp, Tp, D = x_prompt.shape
    Bs, Ts, _ = x_sample.shape
    n_a = w_in_a.shape[0]
    depth = w_o.shape[0]
    H = a_log.shape[1]
    tokd = H * HEAD_DIM
    qkvd = 3 * tokd
    P = cache_k.shape[1]
    nmem = mem_prompt.shape[1]
    mh = cache_mem_k.shape[3]
    memd = mh * HEAD_DIM
    Mp, Ms = Bp * Tp, Bs * Ts
    M = Mp + Ms
    TM = next(t for t in (1024, 512, 256, 128, 64, 32) if M % t == 0)
    TMH = max(TM // 2, 32)
    HR = -(-H // SUBLANES) * SUBLANES

    x = jnp.concatenate([x_prompt.reshape(Mp, D), x_sample.reshape(Ms, D)], axis=0)

    mem_rows = mem_prompt.reshape(Bp * nmem, D)
    wkv = w_mem_kv.astype(BF16)
    pkv = jnp.stack([norm_matmul(mem_rows, norm_mem[l], wkv, layer=l, tm=min(TM, Bp * nmem), tn=memd, out_dtype=F32)
                     for l in range(depth)])
    p_mem_k = pkv[:, :, :memd].reshape(depth, Bp, nmem, mh, HEAD_DIM)
    p_mem_v = pkv[:, :, memd:].reshape(depth, Bp, nmem, mh, HEAD_DIM)
    wo_all = w_o.astype(BF16)
    wup_all = w_up.astype(BF16)
    wdn_all = w_down.astype(BF16)
    wb_all = w_in_b.astype(BF16)

    def hist8(h):
        return jnp.pad(h, ((0, 0), (SUBLANES - (CONV_W - 1), 0), (0, 0)))

    cache_k2 = cache_k.transpose(0, 2, 1, 3)
    cache_v2 = cache_v.transpose(0, 2, 1, 3)

    p_gdn, s_gdn, p_conv, s_conv = [], [], [], []
    lane = jnp.arange(LANES)
    for l in range(depth):
        if l == n_a:
            wkf = jnp.concatenate([w_kvf[:, :tokd], _pad_cols(w_kvf[:, 2 * tokd:], LANES)], axis=1).astype(BF16)
            wv = w_kvf[:, tokd:2 * tokd].astype(BF16)
            pf = jnp.zeros((SUBLANES, LANES), F32).at[0, :H].set(b_f)
            kp2, p_k, lf_p = kv_proj(x, norm_kv, wkf, pf, row0=0, nseq=Bp, T=Tp, tm=TMH, H=H)
            vp2, p_v = kv_proj(x, norm_kv, wv, pf, row0=0, nseq=Bp, T=Tp, tm=TMH, H=H)
            ks2, s_k, lf_s = kv_proj(x, norm_kv, wkf, pf, row0=Mp, nseq=Bs, T=Ts, tm=TMH, H=H)
            vs2, s_v = kv_proj(x, norm_kv, wv, pf, row0=Mp, nseq=Bs, T=Ts, tm=TMH, H=H)
            lf_p = lf_p[:, :H].reshape(Bp, Tp, H)
            lf_s = lf_s[:, :H].reshape(Bs, Ts, H)

            def rows_of(a):
                a = jnp.pad(a.transpose(0, 2, 1), ((0, 0), (0, HR - H), (0, 0)))
                return a.reshape(a.shape[0] * HR, a.shape[2])

            cp_rows = cumsum_lanes(rows_of(lf_p), tb=min(Tp, 512)).reshape(Bp, HR, Tp)[:, :H]
            cp_col = cp_rows.transpose(0, 2, 1).reshape(Mp, H)
            tot = P + Ts
            tot_pad = -(-tot // LANES) * LANES
            lf_all = jnp.pad(jnp.concatenate([cache_logf, lf_s], axis=1), ((0, 0), (0, tot_pad - tot), (0, 0)))
            cs_rows = cumsum_lanes(rows_of(lf_all), tb=LANES).reshape(Bs, HR, tot_pad)[:, :H]
            cs_cache = cs_rows[:, :, :P]
            cs_new = cs_rows[:, :, P:tot]
            cs_col = cs_new.transpose(0, 2, 1).reshape(Ms, H)

        if l < n_a:
            w = w_in_a[l]
            o1 = qkvd + tokd
            w_main = jnp.concatenate([w[:, :o1], w[:, o1 + 2 * H:]], axis=1).astype(BF16)
            w_ab = _pad_cols(w[:, o1:o1 + 2 * H], LANES).astype(BF16)
            pg = jnp.zeros((SUBLANES, LANES), F32)
            pg = pg.at[0, :H].set(a_log[l]).at[1, :H].set(dt_bias[l]).at[2].set((lane < H).astype(F32))
            proj, gb = norm_matmul_side(x, norm_mix_pre[l], w_main, w_ab, pg, _ep_gdn_gates,
                                        tm=TM, tn=_col_tile(w_main.shape[1]))
            cw8 = jnp.pad(conv_w_a[l], ((0, SUBLANES - CONV_W), (0, 0)))
            tok, sp, cp = gdn_mix(proj, gb, hist8(jnp.zeros((Bp, CONV_W - 1, qkvd), F32)), cw8,
                                  jnp.zeros((Bp, H, HEAD_DIM, HEAD_DIM), F32), gdn_norm[l], None,
                                  row0=0, nseq=Bp, T=Tp, L=min(Tp, GDN_CHUNK), H=H)
            tok, ss, cs = gdn_mix(proj, gb, hist8(state_conv[l]), cw8, state_gdn[l], gdn_norm[l], tok,
                                  row0=Mp, nseq=Bs, T=Ts, L=min(Ts, GDN_CHUNK), H=H)
            p_gdn.append(sp)
            s_gdn.append(ss)
            p_conv.append(cp[:, SUBLANES - (CONV_W - 1):])
            s_conv.append(cs[:, SUBLANES - (CONV_W - 1):])
            mq_block = (qkvd + tokd) // memd
        else:
            proj = norm_matmul(x, norm_mix_pre[l], wb_all, layer=l - n_a, tm=TM, tn=_col_tile(wb_all.shape[2]),
                               out_dtype=BF16)
            tfox = min(Tp, 1024)
            tok = fox_prompt(proj, kp2, vp2, cp_col, cp_rows, nseq=Bp, T=Tp, H=H, t=tfox,
                             hb=2, rsub=max(tfox // 256, 1))
            tok = fox_sample(proj, cache_k2, cache_v2, ks2, vs2, cs_col, cs_cache, cs_new, tok,
                             row0=Mp, nseq=Bs, T=Ts, H=H, tk=min(P, 1024))
            mq_block = 2 * tokd // memd
        mem_o = mem_attend(proj, p_mem_k, p_mem_v, l, None, row0=0, rows_per_seq=Tp, tm=min(Tp, 512), sb=1,
                           col_block=mq_block)
        mem_o = mem_attend(proj, cache_mem_k, cache_mem_v, l, mem_o, row0=Mp, rows_per_seq=Ts, tm=Ts,
                           sb=math.gcd(Bs, 8), col_block=mq_block)
        x = out_proj(tok, mem_o, x, wo_all, l, norm_mix_post[l], tm=TMH)
        x = mlp(x, norm_mlp_pre[l], norm_mlp_post[l], wup_all, wdn_all, l, tm=TMH, tf=min(wup_all.shape[2], 1024),
                split_rows=Mp if l == depth - 1 else None)

    y_prompt = x[0].reshape(Bp, Tp, D)
    y_sample = x[1].reshape(Bs, Ts, D)
    return (y_prompt, y_sample, jnp.stack(p_gdn), jnp.stack(p_conv),
            p_k.transpose(0, 2, 1, 3), p_v.transpose(0, 2, 1, 3), lf_p, p_mem_k, p_mem_v,
            jnp.stack(s_gdn), jnp.stack(s_conv),
            s_k.transpose(0, 2, 1, 3), s_v.transpose(0, 2, 1, 3), lf_s)
```
